```python
import math
import jax, jax.numpy as jnp
from jax import lax
import numpy as np

D_MODEL = 1024
BATCH = 8
SEQ = 4096
DEPTH = 2

D_CONV = 256
CONV_GROUPS = 4
CONV_WIDTH = 3
D_GMLP = 256
GMLP_HEADS = 4
GMLP_HEAD_DIM = D_GMLP // GMLP_HEADS
GMLP_CHUNK = 128
HEAD_DIM = 64
NSA_HEADS = 8
NSA_KV_HEADS = 2
NSA_GROUP = NSA_HEADS // NSA_KV_HEADS
D_NSA = NSA_HEADS * HEAD_DIM
D_KV = NSA_KV_HEADS * HEAD_DIM
D_MIX = D_CONV + D_GMLP + D_NSA
D_IN_PROJ = 3 * D_CONV + 2 * D_GMLP + D_NSA + 6 * D_KV + 3 * NSA_HEADS
CMP_LEN = 32
CMP_STRIDE = 16
CMP_HIDDEN = 128
SEL_BLOCK = 64
N_SELECT = 16
WINDOW = 512
Q_BLOCK = 128
D_FF = 2816
ALPHA = (2 * DEPTH) ** 0.25
BETA = (8 * DEPTH) ** -0.25
LN_EPS = 1e-5
NEG = -1e30

kernel_name = "hymba_style_conv_gmlp_nsa_macaron_deepnorm"


def layer_norm(x, g, b):
    xf = x.astype(jnp.float32)
    mu = jnp.mean(xf, axis=-1, keepdims=True)
    var = jnp.mean(jnp.square(xf - mu), axis=-1, keepdims=True)
    return ((xf - mu) * lax.rsqrt(var + LN_EPS) * g + b).astype(x.dtype)


def swiglu(x, w_gate, w_up, w_down):
    return (jax.nn.silu(x @ w_gate) * (x @ w_up)) @ w_down


def masked_softmax(s, mask):
    p = jax.nn.softmax(jnp.where(mask, s.astype(jnp.float32), NEG), axis=-1)
    return p * mask


def short_conv(h, b_gate, c_gate, w_conv):
    z = c_gate * h
    y = lax.conv_general_dilated(
        z, w_conv[:, None, :], window_strides=(1,),
        padding=[(CONV_WIDTH - 1, 0)],
        dimension_numbers=("NWC", "WIO", "NWC"),
        feature_group_count=D_CONV)
    return b_gate * y


def gmlp_spatial(uv, ln_g, ln_b, w_s, b_s):
    bn, s = uv.shape[:2]
    z = jax.nn.gelu(uv)
    u, v = z[..., :D_GMLP], z[..., D_GMLP:]
    v = layer_norm(v, ln_g, ln_b)
    v = v.reshape(bn, s // GMLP_CHUNK, GMLP_CHUNK, GMLP_HEADS, GMLP_HEAD_DIM)
    tril = jnp.tril(jnp.ones((GMLP_CHUNK, GMLP_CHUNK), dtype=bool))
    w = jnp.where(tril[None], w_s, 0.0)
    sv = jnp.einsum("hpq,bcqhd->bcphd", w, v) + b_s.T[:, :, None]
    return u * sv.reshape(bn, s, D_GMLP)


def compress_kv(k, pe, w1, b1, w2):
    s = k.shape[1]
    n_cmp = (s - CMP_LEN) // CMP_STRIDE + 1
    idx = np.arange(n_cmp)[:, None] * CMP_STRIDE + np.arange(CMP_LEN)[None, :]
    kb = k[:, idx]
    h = jax.nn.gelu(jnp.einsum("bnlgd,ldf->bngf", kb + pe[:, None, :], w1) + b1)
    return jnp.einsum("bngf,fe->bnge", h, w2)


def cmp_to_sel_overlap(n_cmp, n_blk):
    st = np.arange(n_cmp) * CMP_STRIDE
    bs = np.arange(n_blk) * SEL_BLOCK
    ov = np.minimum(st[:, None] + CMP_LEN, bs[None, :] + SEL_BLOCK) - np.maximum(st[:, None], bs[None, :])
    return jnp.asarray(np.clip(ov, 0, None) / CMP_LEN, dtype=jnp.float32)


def nsa_attention(q, k_cmp, v_cmp, k_slc, v_slc, k_win, v_win, gate_logits,
                  ck_pe, ck_w1, ck_b1, ck_w2, cv_pe, cv_w1, cv_b1, cv_w2):
    bn, s = q.shape[:2]
    G, R, dh = NSA_KV_HEADS, NSA_GROUP, HEAD_DIM
    q = q.reshape(bn, s, G, R, dh)
    k_cmp, v_cmp, k_slc, v_slc, k_win, v_win = [
        a.reshape(bn, s, G, dh) for a in (k_cmp, v_cmp, k_slc, v_slc, k_win, v_win)]
    scale = dh ** -0.5
    n_cmp = (s - CMP_LEN) // CMP_STRIDE + 1
    n_blk = s // SEL_BLOCK
    n_sel = min(N_SELECT, n_blk)
    t = jnp.arange(s)

    kc = compress_kv(k_cmp, ck_pe, ck_w1, ck_b1, ck_w2)
    vc = compress_kv(v_cmp, cv_pe, cv_w1, cv_b1, cv_w2)
    sc = jnp.einsum("bsgrd,bngd->bgrsn", q, kc) * scale
    cmp_end = jnp.arange(n_cmp) * CMP_STRIDE + CMP_LEN - 1
    pc = masked_softmax(sc, cmp_end[None, :] <= t[:, None])
    o_cmp = jnp.einsum("bgrsn,bngd->bsgrd", pc.astype(vc.dtype), vc)

    imp = jnp.einsum("bgrsn,nj->bgsj", pc, cmp_to_sel_overlap(n_cmp, n_blk))
    j = jnp.arange(n_blk)
    cur = (t // SEL_BLOCK)[:, None]
    valid = j[None, :] <= cur
    forced = (j[None, :] == 0) | (j[None, :] == cur) | (j[None, :] == cur - 1)
    score = jnp.where(valid, imp, NEG)
    score = jnp.where(forced, -NEG, score)
    _, sel = lax.top_k(score, n_sel)

    ks_blk = k_slc.reshape(bn, n_blk, SEL_BLOCK, G, dh).transpose(0, 3, 1, 2, 4)
    vs_blk = v_slc.reshape(bn, n_blk, SEL_BLOCK, G, dh).transpose(0, 3, 1, 2, 4)
    kw_pad = jnp.pad(k_win, ((0, 0), (WINDOW, 0), (0, 0), (0, 0)))
    vw_pad = jnp.pad(v_win, ((0, 0), (WINDOW, 0), (0, 0), (0, 0)))
    bi = jnp.arange(bn)[:, None, None, None]
    gi = jnp.arange(G)[None, :, None, None]

    def query_block(i):
        s0 = i * Q_BLOCK
        qb = lax.dynamic_slice_in_dim(q, s0, Q_BLOCK, axis=1)
        tq = s0 + jnp.arange(Q_BLOCK)
        idx = lax.dynamic_slice_in_dim(sel, s0, Q_BLOCK, axis=2)
        kg = ks_blk[bi, gi, idx]
        vg = vs_blk[bi, gi, idx]
        ss = jnp.einsum("bqgrd,bgqnkd->bgrqnk", qb, kg) * scale
        kpos = idx[..., None] * SEL_BLOCK + jnp.arange(SEL_BLOCK)
        ms = (kpos <= tq[None, None, :, None, None])[:, :, None]
        ps = masked_softmax(ss.reshape(bn, G, R, Q_BLOCK, n_sel * SEL_BLOCK),
                            ms.reshape(bn, G, 1, Q_BLOCK, n_sel * SEL_BLOCK))
        ps = ps.reshape(bn, G, R, Q_BLOCK, n_sel, SEL_BLOCK).astype(vg.dtype)
        o_s = jnp.einsum("bgrqnk,bgqnkd->bqgrd", ps, vg)
        kw = lax.dynamic_slice_in_dim(kw_pad, s0, Q_BLOCK + WINDOW, axis=1)
        vw = lax.dynamic_slice_in_dim(vw_pad, s0, Q_BLOCK + WINDOW, axis=1)
        kp = s0 - WINDOW + jnp.arange(Q_BLOCK + WINDOW)
        dist = tq[:, None] - kp[None, :]
        mw = (dist >= 0) & (dist < WINDOW) & (kp[None, :] >= 0)
        sw = jnp.einsum("bqgrd,bkgd->bgrqk", qb, kw) * scale
        pw = masked_softmax(sw, mw).astype(vw.dtype)
        o_w = jnp.einsum("bgrqk,bkgd->bqgrd", pw, vw)
        return o_s, o_w

    o_s, o_w = lax.map(query_block, jnp.arange(s // Q_BLOCK))
    o_s = o_s.transpose(1, 0, 2, 3, 4, 5).reshape(bn, s, G, R, dh)
    o_w = o_w.transpose(1, 0, 2, 3, 4, 5).reshape(bn, s, G, R, dh)

    g = jax.nn.sigmoid(gate_logits).reshape(bn, s, G, R, 3)
    o = g[..., 0, None] * o_cmp + g[..., 1, None] * o_s + g[..., 2, None] * o_w
    return o.reshape(bn, s, D_NSA)


def setup_inputs(seed: int = 0) -> dict:
    key = jax.random.key(seed)
    keys = iter(jax.random.split(key, 48))
    L = DEPTH

    def nrm(shape, scale):
        return scale * jax.random.normal(next(keys), shape, jnp.float32)

    def gain(shape):
        return 1.0 + nrm(shape, 0.01)

    inp = {}
    inp["x"] = nrm((BATCH, SEQ, D_MODEL), 1.0)
    inp["ffn1_gate"] = nrm((L, D_MODEL, D_FF), D_MODEL ** -0.5)
    inp["ffn1_up"] = nrm((L, D_MODEL, D_FF), D_MODEL ** -0.5)
    inp["ffn1_down"] = nrm((L, D_FF, D_MODEL), BETA * D_FF ** -0.5)
    inp["ln1_g"] = gain((L, D_MODEL))
    inp["ln1_b"] = nrm((L, D_MODEL), 0.01)
    inp["w_in"] = nrm((L, D_MODEL, D_IN_PROJ), D_MODEL ** -0.5)
    inp["conv_w"] = nrm((L, CONV_WIDTH, D_CONV), CONV_WIDTH ** -0.5)
    inp["gmlp_ln_g"] = gain((L, D_GMLP))
    inp["gmlp_ln_b"] = nrm((L, D_GMLP), 0.01)
    inp["gmlp_w"] = nrm((L, GMLP_HEADS, GMLP_CHUNK, GMLP_CHUNK), GMLP_CHUNK ** -0.5)
    inp["gmlp_b"] = gain((L, GMLP_HEADS, GMLP_CHUNK))
    inp["ck_pe"] = nrm((L, CMP_LEN, HEAD_DIM), 0.1)
    inp["ck_w1"] = nrm((L, CMP_LEN, HEAD_DIM, CMP_HIDDEN), (CMP_LEN * HEAD_DIM) ** -0.5)
    inp["ck_b1"] = nrm((L, CMP_HIDDEN), 0.01)
    inp["ck_w2"] = nrm((L, CMP_HIDDEN, HEAD_DIM), CMP_HIDDEN ** -0.5)
    inp["cv_pe"] = nrm((L, CMP_LEN, HEAD_DIM), 0.1)
    inp["cv_w1"] = nrm((L, CMP_LEN, HEAD_DIM, CMP_HIDDEN), (CMP_LEN * HEAD_DIM) ** -0.5)
    inp["cv_b1"] = nrm((L, CMP_HIDDEN), 0.01)
    inp["cv_w2"] = nrm((L, CMP_HIDDEN, HEAD_DIM), CMP_HIDDEN ** -0.5)
    inp["w_out"] = nrm((L, D_MIX, D_MODEL), BETA * D_MIX ** -0.5)
    inp["ln2_g"] = gain((L, D_MODEL))
    inp["ln2_b"] = nrm((L, D_MODEL), 0.01)
    inp["ffn2_gate"] = nrm((L, D_MODEL, D_FF), D_MODEL ** -0.5)
    inp["ffn2_up"] = nrm((L, D_MODEL, D_FF), D_MODEL ** -0.5)
    inp["ffn2_down"] = nrm((L, D_FF, D_MODEL), BETA * D_FF ** -0.5)
    inp["ln3_g"] = gain((L, D_MODEL))
    inp["ln3_b"] = nrm((L, D_MODEL), 0.01)
    return inp


def reference(x, ffn1_gate, ffn1_up, ffn1_down, ln1_g, ln1_b, w_in, conv_w,
              gmlp_ln_g, gmlp_ln_b, gmlp_w, gmlp_b,
              ck_pe, ck_w1, ck_b1, ck_w2, cv_pe, cv_w1, cv_b1, cv_w2,
              w_out, ln2_g, ln2_b, ffn2_gate, ffn2_up, ffn2_down, ln3_g, ln3_b):
    split_points = list(np.cumsum([D_CONV, D_CONV, D_CONV, 2 * D_GMLP, D_NSA,
                                   D_KV, D_KV, D_KV, D_KV, D_KV, D_KV]))
    split_points = [int(p) for p in split_points]
    for l in range(DEPTH):
        x = layer_norm(ALPHA * x + 0.5 * swiglu(x, ffn1_gate[l], ffn1_up[l], ffn1_down[l]),
                       ln1_g[l], ln1_b[l])
        proj = x @ w_in[l]
        (h_a, b_a, c_a, uv, q, kc, vc, ksl, vsl, kw, vw, gl) = jnp.split(proj, split_points, axis=-1)
        y_a = short_conv(h_a, b_a, c_a, conv_w[l])
        y_b = gmlp_spatial(uv, gmlp_ln_g[l], gmlp_ln_b[l], gmlp_w[l], gmlp_b[l])
        y_c = nsa_attention(q, kc, vc, ksl, vsl, kw, vw, gl,
                            ck_pe[l], ck_w1[l], ck_b1[l], ck_w2[l],
                            cv_pe[l], cv_w1[l], cv_b1[l], cv_w2[l])
        mix = jnp.concatenate([y_a, y_b, y_c], axis=-1) @ w_out[l]
        x = layer_norm(ALPHA * x + mix, ln2_g[l], ln2_b[l])
        x = layer_norm(ALPHA * x + 0.5 * swiglu(x, ffn2_gate[l], ffn2_up[l], ffn2_down[l]),
                       ln3_g[l], ln3_b[l])
    return x
```

```python
import functools

import numpy as np
import jax
import jax.numpy as jnp
from jax import lax
from jax.experimental import pallas as pl
from jax.experimental.pallas import tpu as pltpu

F32 = jnp.float32
BF16 = jnp.bfloat16

D_MODEL = 1024
D_FF = 2816
D_CONV = 256
CONV_WIDTH = 3
D_GMLP = 256
GMLP_HEADS = 4
GMLP_HEAD_DIM = D_GMLP // GMLP_HEADS
GMLP_CHUNK = 128
HEAD_DIM = 64
NSA_HEADS = 8
NSA_KV_HEADS = 2
NSA_GROUP = NSA_HEADS // NSA_KV_HEADS
D_NSA = NSA_HEADS * HEAD_DIM
D_KV = NSA_KV_HEADS * HEAD_DIM
CMP_LEN = 32
CMP_STRIDE = 16
CMP_HIDDEN = 128
SEL_BLOCK = 64
SEL_SHIFT = 6
N_SELECT = 16
WINDOW = 512
DEPTH = 2
ALPHA = (2 * DEPTH) ** 0.25
LN_EPS = 1e-5
NEG = -1e30
SCALE = HEAD_DIM ** -0.5

FF_CHUNK = 256
ROW_TILE = 512
CMP_Q_TILE = 512
Q_BLOCK = 128
SEL_KEY_CHUNK = 512
VMEM_LIMIT_FFN = 56 * 1024 * 1024
VMEM_LIMIT = 48 * 1024 * 1024


def _dot(a, b):
    return jnp.dot(a, b, preferred_element_type=F32)


def _dot_nt(a, b):
    return lax.dot_general(a, b, (((1,), (1,)), ((), ())), preferred_element_type=F32)


def _layer_norm(y, g, b):
    mu = jnp.mean(y, axis=-1, keepdims=True)
    d = y - mu
    var = jnp.mean(d * d, axis=-1, keepdims=True)
    return d * lax.rsqrt(var + LN_EPS) * g + b


def _gelu_tanh(x):
    return 0.5 * x * (1.0 + jnp.tanh(0.7978845608028654 * (x + 0.044715 * (x * x * x))))


def _ffn_ln_kernel(x_ref, wg_ref, wu_ref, wd_ref, g_ref, b_ref, o_ref, acc_ref):
    x = x_ref[...]
    xb = x.astype(BF16)
    acc_ref[...] = jnp.zeros_like(acc_ref)

    def body(c, carry):
        gate = _dot(xb, wg_ref[c])
        up = _dot(xb, wu_ref[c])
        h = (gate * jax.nn.sigmoid(gate) * up).astype(BF16)
        acc_ref[...] += _dot(h, wd_ref[c])
        return carry

    lax.fori_loop(0, D_FF // FF_CHUNK, body, 0)
    y = ALPHA * x + 0.5 * acc_ref[...]
    o_ref[...] = _layer_norm(y, g_ref[...], b_ref[...])


def _ffn_ln(x2d, wg, wu, wd, g, b):
    rows = x2d.shape[0]
    nc = D_FF // FF_CHUNK
    wgc = wg.astype(BF16).reshape(D_MODEL, nc, FF_CHUNK).transpose(1, 0, 2)
    wuc = wu.astype(BF16).reshape(D_MODEL, nc, FF_CHUNK).transpose(1, 0, 2)
    wdc = wd.astype(BF16).reshape(nc, FF_CHUNK, D_MODEL)
    resident = functools.partial(pl.BlockSpec, pipeline_mode=pl.Buffered(1))
    return pl.pallas_call(
        _ffn_ln_kernel,
        grid=(rows // ROW_TILE,),
        in_specs=[
            pl.BlockSpec((ROW_TILE, D_MODEL), lambda i: (i, 0)),
            resident((nc, D_MODEL, FF_CHUNK), lambda i: (0, 0, 0)),
            resident((nc, D_MODEL, FF_CHUNK), lambda i: (0, 0, 0)),
            resident((nc, FF_CHUNK, D_MODEL), lambda i: (0, 0, 0)),
            pl.BlockSpec((1, D_MODEL), lambda i: (0, 0)),
            pl.BlockSpec((1, D_MODEL), lambda i: (0, 0)),
        ],
        out_specs=pl.BlockSpec((ROW_TILE, D_MODEL), lambda i: (i, 0)),
        out_shape=jax.ShapeDtypeStruct((rows, D_MODEL), F32),
        scratch_shapes=[pltpu.VMEM((ROW_TILE, D_MODEL), F32)],
        compiler_params=pltpu.CompilerParams(
            dimension_semantics=("parallel",), vmem_limit_bytes=VMEM_LIMIT_FFN),
        name="ffn_ln",
    )(x2d, wgc, wuc, wdc, g.reshape(1, D_MODEL), b.reshape(1, D_MODEL))


def _proj_kernel(tiles_per_seq, x_ref, xh_ref, wconv_ref, wuv_ref, wq_ref, wkv_ref, wgl_ref,
                 convw_ref, glg_ref, glb_ref, gw_ref, gb_ref,
                 yab_ref, q_ref, kc_ref, vc_ref, kv_ref, gate_ref):
    i = pl.program_id(0)
    tm = x_ref.shape[0]
    xb = x_ref[...].astype(BF16)

    hbc = _dot(xb, wconv_ref[...])
    z = hbc[:, 2 * D_CONV:] * hbc[:, :D_CONV]
    hbc_halo = _dot(xh_ref[...].astype(BF16), wconv_ref[...])
    z_halo = hbc_halo[:, 2 * D_CONV:] * hbc_halo[:, :D_CONV]
    z_halo = jnp.where(i % tiles_per_seq == 0, 0.0, z_halo)
    row = lax.broadcasted_iota(jnp.int32, (tm, D_CONV), 0)
    z1 = jnp.where(row == 0, z_halo[7:8], pltpu.roll(z, 1, 0))
    z2 = jnp.where(row == 0, z_halo[6:7], jnp.where(row == 1, z_halo[7:8], pltpu.roll(z, 2, 0)))
    cw = convw_ref[...]
    y_a = hbc[:, D_CONV:2 * D_CONV] * (cw[0:1] * z2 + cw[1:2] * z1 + cw[2:3] * z)

    zz = _gelu_tanh(_dot(xb, wuv_ref[...]))
    u = zz[:, :D_GMLP]
    v = _layer_norm(zz[:, D_GMLP:], glg_ref[...], glb_ref[...]).astype(BF16)
    pr = lax.broadcasted_iota(jnp.int32, (GMLP_CHUNK, GMLP_CHUNK), 0)
    pc = lax.broadcasted_iota(jnp.int32, (GMLP_CHUNK, GMLP_CHUNK), 1)
    w_tril = [jnp.where(pr >= pc, gw_ref[h], 0.0).astype(BF16) for h in range(GMLP_HEADS)]
    sv_chunks = []
    for c in range(tm // GMLP_CHUNK):
        vch = v[c * GMLP_CHUNK:(c + 1) * GMLP_CHUNK]
        parts = [_dot(w_tril[h], vch[:, h * GMLP_HEAD_DIM:(h + 1) * GMLP_HEAD_DIM])
                 for h in range(GMLP_HEADS)]
        sv_chunks.append(jnp.concatenate(parts, axis=1) + gb_ref[...])
    y_b = u * jnp.concatenate(sv_chunks, axis=0)

    yab_ref[...] = jnp.concatenate([y_a, y_b], axis=1).astype(BF16)
    q_ref[...] = (_dot(xb, wq_ref[...]) * SCALE).astype(BF16)
    kv = _dot(xb, wkv_ref[...])
    kc_ref[...] = kv[:, :D_KV].astype(BF16)
    vc_ref[...] = kv[:, D_KV:2 * D_KV].astype(BF16)
    kv_ref[...] = kv[:, 2 * D_KV:].astype(BF16)
    gate_ref[...] = jax.nn.sigmoid(_dot(xb, wgl_ref[...]))


def _proj(x2d, seq, w_in, conv_w, gln_g, gln_b, gmlp_w, gmlp_b):
    rows = x2d.shape[0]
    tm = ROW_TILE
    w = w_in.astype(BF16)
    o = 0
    wconv = w[:, o:o + 3 * D_CONV]; o += 3 * D_CONV
    wuv = w[:, o:o + 2 * D_GMLP]; o += 2 * D_GMLP
    wq = w[:, o:o + D_NSA]; o += D_NSA
    wkv = w[:, o:o + 6 * D_KV]; o += 6 * D_KV
    n_gate = 3 * NSA_HEADS
    wgl = jnp.pad(w[:, o:o + n_gate], ((0, 0), (0, 128 - n_gate)))
    gb_exp = jnp.repeat(gmlp_b.T, GMLP_HEAD_DIM, axis=1)
    const = lambda shape: pl.BlockSpec(shape, lambda i: (0,) * len(shape))
    row_spec = lambda width: pl.BlockSpec((tm, width), lambda i: (i, 0))
    return pl.pallas_call(
        functools.partial(_proj_kernel, seq // tm),
        grid=(rows // tm,),
        in_specs=[
            row_spec(D_MODEL),
            pl.BlockSpec((8, D_MODEL), lambda i: (jnp.maximum(i * (tm // 8) - 1, 0), 0)),
            const(wconv.shape), const(wuv.shape), const(wq.shape), const(wkv.shape), const(wgl.shape),
            const((CONV_WIDTH, D_CONV)), const((1, D_GMLP)), const((1, D_GMLP)),
            const((GMLP_HEADS, GMLP_CHUNK, GMLP_CHUNK)), const((GMLP_CHUNK, D_GMLP)),
        ],
        out_specs=[row_spec(D_CONV + D_GMLP), row_spec(D_NSA), row_spec(D_KV), row_spec(D_KV),
                   row_spec(4 * D_KV), row_spec(128)],
        out_shape=[
            jax.ShapeDtypeStruct((rows, D_CONV + D_GMLP), BF16),
            jax.ShapeDtypeStruct((rows, D_NSA), BF16),
            jax.ShapeDtypeStruct((rows, D_KV), BF16),
            jax.ShapeDtypeStruct((rows, D_KV), BF16),
            jax.ShapeDtypeStruct((rows, 4 * D_KV), BF16),
            jax.ShapeDtypeStruct((rows, 128), F32),
        ],
        compiler_params=pltpu.CompilerParams(
            dimension_semantics=("parallel",), vmem_limit_bytes=VMEM_LIMIT),
        name="in_proj",
    )(x2d, x2d, wconv, wuv, wq, wkv, wgl, conv_w, gln_g.reshape(1, D_GMLP), gln_b.reshape(1, D_GMLP),
      gmlp_w, gb_exp)


def _compress_kernel(k_ref, v_ref, w1k_ref, w1v_ref, pek_ref, pev_ref, b1k_ref, b1v_ref,
                     w2k_ref, w2v_ref, kc_ref, vc_ref):
    def one(x2d, w1, pe2, b1, w2):
        n = x2d.shape[0]
        r = _dot(x2d, w1)
        cst = _dot(pe2, w1)
        outs = []
        for g in range(NSA_KV_HEADS):
            lo = slice(g * CMP_HIDDEN, (g + 1) * CMP_HIDDEN)
            hi = slice((NSA_KV_HEADS + g) * CMP_HIDDEN, (NSA_KV_HEADS + g + 1) * CMP_HIDDEN)
            first = r[:, lo] + cst[0:1, lo]
            second = r[:, hi] + cst[1:2, hi]
            h = _gelu_tanh(first + pltpu.roll(second, n - 1, 0) + b1)
            outs.append(_dot(h.astype(BF16), w2))
        return jnp.concatenate(outs, axis=1)

    kc_ref[0] = one(k_ref[0], w1k_ref[...], pek_ref[...], b1k_ref[...], w2k_ref[...])
    vc_ref[0] = one(v_ref[0], w1v_ref[...], pev_ref[...], b1v_ref[...], w2v_ref[...])


def _expand_w1(w1):
    half = CMP_LEN // 2
    eye = jnp.eye(NSA_KV_HEADS, dtype=w1.dtype)
    w = w1.reshape(2, half, HEAD_DIM, CMP_HIDDEN)
    e = jnp.einsum("aldf,hg->lhdagf", w, eye)
    return e.reshape(half * NSA_KV_HEADS * HEAD_DIM, 2 * NSA_KV_HEADS * CMP_HIDDEN).astype(BF16)


def _expand_pe(pe):
    half = CMP_LEN // 2
    p = pe.reshape(2, half, 1, HEAD_DIM)
    p = jnp.broadcast_to(p, (2, half, NSA_KV_HEADS, HEAD_DIM)).reshape(2, -1)
    return jnp.pad(p, ((0, 6), (0, 0))).astype(BF16)


def _compress(kc2d, vc2d, batch, seq, ck_pe, ck_w1, ck_b1, ck_w2, cv_pe, cv_w1, cv_b1, cv_w2):
    n = seq // CMP_STRIDE
    width = CMP_STRIDE * D_KV
    k3 = kc2d.reshape(batch, n, width)
    v3 = vc2d.reshape(batch, n, width)
    const = lambda shape: pl.BlockSpec(shape, lambda b: (0,) * len(shape))
    blk = pl.BlockSpec((1, n, width), lambda b: (b, 0, 0))
    out = pl.BlockSpec((1, n, D_KV), lambda b: (b, 0, 0))
    wshape = (width, 2 * NSA_KV_HEADS * CMP_HIDDEN)
    return pl.pallas_call(
        _compress_kernel,
        grid=(batch,),
        in_specs=[blk, blk, const(wshape), const(wshape), const((8, width)), const((8, width)),
                  const((1, CMP_HIDDEN)), const((1, CMP_HIDDEN)),
                  const((CMP_HIDDEN, HEAD_DIM)), const((CMP_HIDDEN, HEAD_DIM))],
        out_specs=[out, out],
        out_shape=[jax.ShapeDtypeStruct((batch, n, D_KV), F32)] * 2,
        compiler_params=pltpu.CompilerParams(
            dimension_semantics=("parallel",), vmem_limit_bytes=VMEM_LIMIT),
        name="compress_kv",
    )(k3, v3, _expand_w1(ck_w1), _expand_w1(cv_w1), _expand_pe(ck_pe), _expand_pe(cv_pe),
      ck_b1.reshape(1, CMP_HIDDEN), cv_b1.reshape(1, CMP_HIDDEN),
      ck_w2.astype(BF16), cv_w2.astype(BF16))


def _cmp_attn_kernel(q_ref, kc_ref, vc_ref, ovt_ref, ocmp_ref, sel_ref):
    tq = q_ref.shape[0]
    n_cmp = kc_ref.shape[1]
    n_blk = ovt_ref.shape[0]
    t0 = pl.program_id(1) * tq
    q = q_ref[...]
    kc = kc_ref[0].astype(BF16)
    vc = vc_ref[0].astype(BF16)
    ovt = ovt_ref[...]
    tpos = t0 + lax.broadcasted_iota(jnp.int32, (tq, n_cmp), 0)
    nidx = lax.broadcasted_iota(jnp.int32, (tq, n_cmp), 1)
    visible = nidx * CMP_STRIDE + (CMP_LEN - 1) <= tpos

    jj = lax.broadcasted_iota(jnp.int32, (n_blk, tq), 0)
    cur = jnp.right_shift(t0 + lax.broadcasted_iota(jnp.int32, (n_blk, tq), 1), SEL_SHIFT)
    valid = jj <= cur
    forced = (jj == 0) | (jj == cur) | (jj == cur - 1)
    sub8 = lax.broadcasted_iota(jnp.int32, (8, tq), 0)

    outs = []
    sel_t = []
    for g in range(NSA_KV_HEADS):
        kg = kc[:, g * HEAD_DIM:(g + 1) * HEAD_DIM]
        vg = vc[:, g * HEAD_DIM:(g + 1) * HEAD_DIM]
        psum = jnp.zeros((tq, n_cmp), F32)
        for r in range(NSA_GROUP):
            h = g * NSA_GROUP + r
            s = jnp.where(visible, _dot_nt(q[:, h * HEAD_DIM:(h + 1) * HEAD_DIM], kg), NEG)
            e = jnp.exp(s - jnp.max(s, axis=-1, keepdims=True))
            p = jnp.where(visible, e / jnp.sum(e, axis=-1, keepdims=True), 0.0)
            outs.append(_dot(p.astype(BF16), vg))
            psum = psum + p
        hi = psum.astype(BF16)
        lo = (psum - hi.astype(F32)).astype(BF16)
        imp_t = _dot_nt(ovt, hi) + _dot_nt(ovt, lo)
        score = jnp.where(forced, -NEG, jnp.where(valid, imp_t, NEG))
        groups = [score[8 * b:8 * b + 8] for b in range(n_blk // 8)]
        ranks = [jnp.zeros((8, tq), F32) for _ in groups]
        for i in range(n_blk):
            row = jnp.broadcast_to(score[i:i + 1, :], (8, tq))
            for b, blk in enumerate(groups):
                if 8 * b > i:
                    before = jnp.where(row >= blk, 1.0, 0.0)
                elif 8 * b + 7 < i:
                    before = jnp.where(row > blk, 1.0, 0.0)
                else:
                    before = jnp.where(sub8 + 8 * b > i, jnp.where(row >= blk, 1.0, 0.0),
                                       jnp.where(row > blk, 1.0, 0.0))
                ranks[b] = ranks[b] + before
        rank = jnp.concatenate(ranks, axis=0)
        sel_t.append(jnp.where(rank < float(min(N_SELECT, n_blk)), 1.0, 0.0))
    ocmp_ref[...] = jnp.concatenate(outs, axis=1)
    sel_ref[...] = jnp.concatenate(sel_t, axis=0).T.astype(BF16)


def _overlap_t(n_cmp_pad, n_blk):
    st = np.arange(n_cmp_pad) * CMP_STRIDE
    bs = np.arange(n_blk) * SEL_BLOCK
    ov = np.minimum(st[:, None] + CMP_LEN, bs[None, :] + SEL_BLOCK) - np.maximum(st[:, None], bs[None, :])
    return jnp.asarray((np.clip(ov, 0, None) / CMP_LEN).T, dtype=BF16)


def _cmp_attn(q2d, kc, vc, batch, seq):
    tq = CMP_Q_TILE
    n_cmp = kc.shape[1]
    n_blk = seq // SEL_BLOCK
    per = seq // tq
    return pl.pallas_call(
        _cmp_attn_kernel,
        grid=(batch, per),
        in_specs=[
            pl.BlockSpec((tq, D_NSA), lambda b, j: (b * per + j, 0)),
            pl.BlockSpec((1, n_cmp, D_KV), lambda b, j: (b, 0, 0)),
            pl.BlockSpec((1, n_cmp, D_KV), lambda b, j: (b, 0, 0)),
            pl.BlockSpec((n_blk, n_cmp), lambda b, j: (0, 0)),
        ],
        out_specs=[pl.BlockSpec((tq, D_NSA), lambda b, j: (b * per + j, 0)),
                   pl.BlockSpec((tq, NSA_KV_HEADS * n_blk), lambda b, j: (b * per + j, 0))],
        out_shape=[jax.ShapeDtypeStruct((batch * seq, D_NSA), F32),
                   jax.ShapeDtypeStruct((batch * seq, NSA_KV_HEADS * n_blk), BF16)],
        compiler_params=pltpu.CompilerParams(
            dimension_semantics=("parallel", "parallel"), vmem_limit_bytes=VMEM_LIMIT),
        name="cmp_attn_select",
    )(q2d, kc, vc, _overlap_t(n_cmp, n_blk))


def _sel_win_kernel(q_ref, kv_ref, sel_ref, ocmp_ref, gate_ref, out_ref, m_ref, l_ref, acc_ref):
    qb = q_ref.shape[0]
    n_blk = sel_ref.shape[1] // NSA_KV_HEADS
    kc = SEL_KEY_CHUNK
    s0 = pl.program_id(1) * qb
    q = q_ref[...]
    sel = sel_ref[...]
    gate = gate_ref[...]
    ocmp = ocmp_ref[...]
    n_chunks = (s0 + qb + kc - 1) // kc
    win_keys = WINDOW + qb
    win_start = pl.multiple_of(jnp.maximum(s0 - WINDOW, 0), qb)

    pieces = []
    for g in range(NSA_KV_HEADS):
        heads = range(g * NSA_GROUP, (g + 1) * NSA_GROUP)
        qg = jnp.concatenate([q[:, h * HEAD_DIM:(h + 1) * HEAD_DIM] for h in heads], axis=0)
        selg = sel[:, g * n_blk:(g + 1) * n_blk]
        col = lambda kind: slice((kind * NSA_KV_HEADS + g) * HEAD_DIM,
                                 (kind * NSA_KV_HEADS + g + 1) * HEAD_DIM)

        m_ref[...] = jnp.full(m_ref.shape, NEG, F32)
        l_ref[...] = jnp.zeros(l_ref.shape, F32)
        acc_ref[...] = jnp.zeros(acc_ref.shape, F32)

        def body(c, carry):
            k0 = pl.multiple_of(c * kc, kc)
            ks = kv_ref[pl.ds(k0, kc), col(0)]
            vs = kv_ref[pl.ds(k0, kc), col(1)]
            blk_of_key = jnp.right_shift(k0 + lax.broadcasted_iota(jnp.int32, (n_blk, kc), 1), SEL_SHIFT)
            expand = jnp.where(blk_of_key == lax.broadcasted_iota(jnp.int32, (n_blk, kc), 0),
                               1.0, 0.0).astype(BF16)
            chosen = _dot(selg, expand)
            kpos = k0 + lax.broadcasted_iota(jnp.int32, (qb, kc), 1)
            tpos = s0 + lax.broadcasted_iota(jnp.int32, (qb, kc), 0)
            bias = jnp.where(kpos <= tpos, jnp.where(chosen > 0.5, 0.0, NEG), NEG)
            s = _dot_nt(qg, ks) + jnp.concatenate([bias] * NSA_GROUP, axis=0)
            m_old = m_ref[...]
            m_new = jnp.maximum(m_old, jnp.max(s, axis=-1, keepdims=True))
            alpha = jnp.exp(m_old - m_new)
            p = jnp.exp(s - m_new)
            l_ref[...] = alpha * l_ref[...] + jnp.sum(p, axis=-1, keepdims=True)
            acc_ref[...] = alpha * acc_ref[...] + _dot(p.astype(BF16), vs)
            m_ref[...] = m_new
            return carry

        lax.fori_loop(0, n_chunks, body, 0)
        o_s = acc_ref[...] / l_ref[...]

        kw = kv_ref[pl.ds(win_start, win_keys), col(2)]
        vw = kv_ref[pl.ds(win_start, win_keys), col(3)]
        dist = (s0 + lax.broadcasted_iota(jnp.int32, (qb, win_keys), 0)) - (
            win_start + lax.broadcasted_iota(jnp.int32, (qb, win_keys), 1))
        wbias = jnp.where((dist >= 0) & (dist < WINDOW), 0.0, NEG)
        sw = _dot_nt(qg, kw) + jnp.concatenate([wbias] * NSA_GROUP, axis=0)
        pw = jnp.exp(sw - jnp.max(sw, axis=-1, keepdims=True))
        o_w = _dot(pw.astype(BF16), vw) / jnp.sum(pw, axis=-1, keepdims=True)

        for r, h in enumerate(heads):
            rows = slice(r * qb, (r + 1) * qb)
            cols = slice(h * HEAD_DIM, (h + 1) * HEAD_DIM)
            pieces.append(gate[:, 3 * h:3 * h + 1] * ocmp[:, cols]
                          + gate[:, 3 * h + 1:3 * h + 2] * o_s[rows]
                          + gate[:, 3 * h + 2:3 * h + 3] * o_w[rows])
    out_ref[...] = jnp.concatenate(pieces, axis=1).astype(BF16)


def _sel_win(q2d, kv2d, sel2d, ocmp2d, gate2d, batch, seq):
    qb = Q_BLOCK
    per = seq // qb
    row = lambda width: pl.BlockSpec((qb, width), lambda b, i: (b * per + i, 0))
    rows_g = NSA_GROUP * qb
    return pl.pallas_call(
        _sel_win_kernel,
        grid=(batch, per),
        in_specs=[row(D_NSA),
                  pl.BlockSpec((seq, 4 * D_KV), lambda b, i: (b, 0)),
                  row(sel2d.shape[1]), row(D_NSA), row(128)],
        out_specs=row(D_NSA),
        out_shape=jax.ShapeDtypeStruct((batch * seq, D_NSA), BF16),
        scratch_shapes=[pltpu.VMEM((rows_g, 1), F32), pltpu.VMEM((rows_g, 1), F32),
                        pltpu.VMEM((rows_g, HEAD_DIM), F32)],
        compiler_params=pltpu.CompilerParams(
            dimension_semantics=("parallel", "arbitrary"), vmem_limit_bytes=VMEM_LIMIT),
        name="sel_win_attn",
    )(q2d, kv2d, sel2d, ocmp2d, gate2d)


def _out_ln_kernel(x_ref, yab_ref, yc_ref, wab_ref, wc_ref, g_ref, b_ref, o_ref):
    mix = _dot(yab_ref[...], wab_ref[...]) + _dot(yc_ref[...], wc_ref[...])
    o_ref[...] = _layer_norm(ALPHA * x_ref[...] + mix, g_ref[...], b_ref[...])


def _out_ln(x2d, yab, yc, w_out, g, b):
    rows = x2d.shape[0]
    tm = ROW_TILE
    w = w_out.astype(BF16)
    d_ab = D_CONV + D_GMLP
    row = lambda width: pl.BlockSpec((tm, width), lambda i: (i, 0))
    const = lambda shape: pl.BlockSpec(shape, lambda i: (0,) * len(shape))
    return pl.pallas_call(
        _out_ln_kernel,
        grid=(rows // tm,),
        in_specs=[row(D_MODEL), row(d_ab), row(D_NSA), const((d_ab, D_MODEL)), const((D_NSA, D_MODEL)),
                  const((1, D_MODEL)), const((1, D_MODEL))],
        out_specs=row(D_MODEL),
        out_shape=jax.ShapeDtypeStruct((rows, D_MODEL), F32),
        compiler_params=pltpu.CompilerParams(
            dimension_semantics=("parallel",), vmem_limit_bytes=VMEM_LIMIT),
        name="out_proj_ln",
    )(x2d, yab, yc, w[:d_ab], w[d_ab:], g.reshape(1, D_MODEL), b.reshape(1, D_MODEL))


def kernel(x, ffn1_gate, ffn1_up, ffn1_down, ln1_g, ln1_b, w_in, conv_w, gmlp_ln_g, gmlp_ln_b, gmlp_w, gmlp_b, ck_pe, ck_w1, ck_b1, ck_w2, cv_pe, cv_w1, cv_b1, cv_w2, w_out, ln2_g, ln2_b, ffn2_gate, ffn2_up, ffn2_down, ln3_g, ln3_b):
    batch, seq, d = x.shape
    assert d == D_MODEL and seq % max(ROW_TILE, CMP_Q_TILE, SEL_KEY_CHUNK) == 0
    h = x.reshape(batch * seq, d)
    for l in range(ffn1_gate.shape[0]):
        h = _ffn_ln(h, ffn1_gate[l], ffn1_up[l], ffn1_down[l], ln1_g[l], ln1_b[l])
        yab, q, kcr, vcr, kv, gate = _proj(h, seq, w_in[l], conv_w[l], gmlp_ln_g[l], gmlp_ln_b[l],
                                           gmlp_w[l], gmlp_b[l])
        kc, vc = _compress(kcr, vcr, batch, seq, ck_pe[l], ck_w1[l], ck_b1[l], ck_w2[l],
                           cv_pe[l], cv_w1[l], cv_b1[l], cv_w2[l])
        ocmp, sel = _cmp_attn(q, kc, vc, batch, seq)
        yc = _sel_win(q, kv, sel, ocmp, gate, batch, seq)
        h = _out_ln(h, yab, yc, w_out[l], ln2_g[l], ln2_b[l])
        h = _ffn_ln(h, ffn2_gate[l], ffn2_up[l], ffn2_down[l], ln3_g[l], ln3_b[l])
    return h.reshape(batch, seq, d)
```

```python
import functools

import numpy as np
import jax
import jax.numpy as jnp
from jax import lax
from jax.experimental import pallas as pl
from jax.experimental.pallas import tpu as pltpu

F32 = jnp.float32
BF16 = jnp.bfloat16

D_MODEL = 1024
D_FF = 2816
D_CONV = 256
CONV_WIDTH = 3
D_GMLP = 256
GMLP_HEADS = 4
GMLP_HEAD_DIM = D_GMLP // GMLP_HEADS
GMLP_CHUNK = 128
HEAD_DIM = 64
NSA_HEADS = 8
NSA_KV_HEADS = 2
NSA_GROUP = NSA_HEADS // NSA_KV_HEADS
D_NSA = NSA_HEADS * HEAD_DIM
D_KV = NSA_KV_HEADS * HEAD_DIM
CMP_LEN = 32
CMP_STRIDE = 16
CMP_HIDDEN = 128
SEL_BLOCK = 64
SEL_SHIFT = 6
N_SELECT = 16
WINDOW = 512
DEPTH = 2
ALPHA = (2 * DEPTH) ** 0.25
LN_EPS = 1e-5
NEG = -1e30
LOG2E = 1.4426950408889634
Q_SCALE = HEAD_DIM ** -0.5 * LOG2E

FF_CHUNK = 256
ROW_TILE = 512
CMP_Q_TILE = 512
Q_BLOCK = 128
SEL_KEY_CHUNK = 512
VMEM_LIMIT_FFN = 56 * 1024 * 1024
VMEM_LIMIT = 48 * 1024 * 1024


def _dot(a, b):
    return jnp.dot(a, b, preferred_element_type=F32)


def _dot_nt(a, b):
    return lax.dot_general(a, b, (((1,), (1,)), ((), ())), preferred_element_type=F32)


def _layer_norm(y, g, b):
    mu = jnp.mean(y, axis=-1, keepdims=True)
    d = y - mu
    var = jnp.mean(d * d, axis=-1, keepdims=True)
    return d * lax.rsqrt(var + LN_EPS) * g + b


def _gelu_tanh(x):
    return 0.5 * x * (1.0 + jnp.tanh(0.7978845608028654 * (x + 0.044715 * (x * x * x))))


def _ffn_ln_kernel(x_ref, wg_ref, wu_ref, wd_ref, g_ref, b_ref, o_ref, acc_ref):
    x = x_ref[...]
    xb = x.astype(BF16)
    acc_ref[...] = jnp.zeros_like(acc_ref)

    def body(c, carry):
        gate = _dot(xb, wg_ref[c])
        up = _dot(xb, wu_ref[c])
        h = (gate * jax.nn.sigmoid(gate) * up).astype(BF16)
        acc_ref[...] += _dot(h, wd_ref[c])
        return carry

    lax.fori_loop(0, D_FF // FF_CHUNK, body, 0)
    y = ALPHA * x + 0.5 * acc_ref[...]
    o_ref[...] = _layer_norm(y, g_ref[...], b_ref[...])


def _ffn_ln(x2d, wg, wu, wd, g, b):
    rows = x2d.shape[0]
    nc = D_FF // FF_CHUNK
    wgc = wg.astype(BF16).reshape(D_MODEL, nc, FF_CHUNK).transpose(1, 0, 2)
    wuc = wu.astype(BF16).reshape(D_MODEL, nc, FF_CHUNK).transpose(1, 0, 2)
    wdc = wd.astype(BF16).reshape(nc, FF_CHUNK, D_MODEL)
    resident = functools.partial(pl.BlockSpec, pipeline_mode=pl.Buffered(1))
    return pl.pallas_call(
        _ffn_ln_kernel,
        grid=(rows // ROW_TILE,),
        in_specs=[
            pl.BlockSpec((ROW_TILE, D_MODEL), lambda i: (i, 0)),
            resident((nc, D_MODEL, FF_CHUNK), lambda i: (0, 0, 0)),
            resident((nc, D_MODEL, FF_CHUNK), lambda i: (0, 0, 0)),
            resident((nc, FF_CHUNK, D_MODEL), lambda i: (0, 0, 0)),
            pl.BlockSpec((1, D_MODEL), lambda i: (0, 0)),
            pl.BlockSpec((1, D_MODEL), lambda i: (0, 0)),
        ],
        out_specs=pl.BlockSpec((ROW_TILE, D_MODEL), lambda i: (i, 0)),
        out_shape=jax.ShapeDtypeStruct((rows, D_MODEL), F32),
        scratch_shapes=[pltpu.VMEM((ROW_TILE, D_MODEL), F32)],
        compiler_params=pltpu.CompilerParams(
            dimension_semantics=("parallel",), vmem_limit_bytes=VMEM_LIMIT_FFN),
        name="ffn_ln",
    )(x2d, wgc, wuc, wdc, g.reshape(1, D_MODEL), b.reshape(1, D_MODEL))


def _proj_kernel(tiles_per_seq, x_ref, xh_ref, wconv_ref, wuv_ref, wq_ref, wkv_ref, wgl_ref,
                 convw_ref, glg_ref, glb_ref, gw_ref, gb_ref,
                 yab_ref, qt_ref, kc_ref, vc_ref, kk_ref, vvt_ref, gatet_ref):
    i = pl.program_id(0)
    tm = x_ref.shape[0]
    xb = x_ref[...].astype(BF16)

    hbc = _dot(xb, wconv_ref[...])
    z = hbc[:, 2 * D_CONV:] * hbc[:, :D_CONV]
    hbc_halo = _dot(xh_ref[...].astype(BF16), wconv_ref[...])
    z_halo = hbc_halo[:, 2 * D_CONV:] * hbc_halo[:, :D_CONV]
    z_halo = jnp.where(i % tiles_per_seq == 0, 0.0, z_halo)
    row = lax.broadcasted_iota(jnp.int32, (tm, D_CONV), 0)
    z1 = jnp.where(row == 0, z_halo[7:8], pltpu.roll(z, 1, 0))
    z2 = jnp.where(row == 0, z_halo[6:7], jnp.where(row == 1, z_halo[7:8], pltpu.roll(z, 2, 0)))
    cw = convw_ref[...]
    y_a = hbc[:, D_CONV:2 * D_CONV] * (cw[0:1] * z2 + cw[1:2] * z1 + cw[2:3] * z)

    zz = _gelu_tanh(_dot(xb, wuv_ref[...]))
    u = zz[:, :D_GMLP]
    v = _layer_norm(zz[:, D_GMLP:], glg_ref[...], glb_ref[...]).astype(BF16)
    pr = lax.broadcasted_iota(jnp.int32, (GMLP_CHUNK, GMLP_CHUNK), 0)
    pc = lax.broadcasted_iota(jnp.int32, (GMLP_CHUNK, GMLP_CHUNK), 1)
    w_tril = [jnp.where(pr >= pc, gw_ref[h], 0.0).astype(BF16) for h in range(GMLP_HEADS)]
    sv_chunks = []
    for c in range(tm // GMLP_CHUNK):
        vch = v[c * GMLP_CHUNK:(c + 1) * GMLP_CHUNK]
        parts = [_dot(w_tril[h], vch[:, h * GMLP_HEAD_DIM:(h + 1) * GMLP_HEAD_DIM])
                 for h in range(GMLP_HEADS)]
        sv_chunks.append(jnp.concatenate(parts, axis=1) + gb_ref[...])
    y_b = u * jnp.concatenate(sv_chunks, axis=0)

    yab_ref[...] = jnp.concatenate([y_a, y_b], axis=1).astype(BF16)
    qt_ref[...] = (_dot(xb, wq_ref[...]) * Q_SCALE).T.astype(BF16)
    kv = _dot(xb, wkv_ref[...])
    kc_ref[...] = kv[:, :D_KV].astype(BF16)
    vc_ref[...] = kv[:, D_KV:2 * D_KV].astype(BF16)
    kk_ref[...] = jnp.concatenate([kv[:, 2 * D_KV:3 * D_KV], kv[:, 4 * D_KV:5 * D_KV]], axis=1).astype(BF16)
    vvt_ref[...] = jnp.concatenate([kv[:, 3 * D_KV:4 * D_KV], kv[:, 5 * D_KV:]], axis=1).T.astype(BF16)
    gatet_ref[...] = jax.nn.sigmoid(_dot(xb, wgl_ref[...])).T


def _proj(x2d, seq, w_in, conv_w, gln_g, gln_b, gmlp_w, gmlp_b):
    rows = x2d.shape[0]
    tm = ROW_TILE
    w = w_in.astype(BF16)
    o = 0
    wconv = w[:, o:o + 3 * D_CONV]; o += 3 * D_CONV
    wuv = w[:, o:o + 2 * D_GMLP]; o += 2 * D_GMLP
    wq = w[:, o:o + D_NSA]; o += D_NSA
    wkv = w[:, o:o + 6 * D_KV]; o += 6 * D_KV
    n_gate = 3 * NSA_HEADS
    wgl = jnp.pad(w[:, o:o + n_gate], ((0, 0), (0, 128 - n_gate)))
    gb_exp = jnp.repeat(gmlp_b.T, GMLP_HEAD_DIM, axis=1)
    const = lambda shape: pl.BlockSpec(shape, lambda i: (0,) * len(shape))
    row_spec = lambda width: pl.BlockSpec((tm, width), lambda i: (i, 0))
    col_spec = lambda height: pl.BlockSpec((height, tm), lambda i: (0, i))
    return pl.pallas_call(
        functools.partial(_proj_kernel, seq // tm),
        grid=(rows // tm,),
        in_specs=[
            row_spec(D_MODEL),
            pl.BlockSpec((8, D_MODEL), lambda i: (jnp.maximum(i * (tm // 8) - 1, 0), 0)),
            const(wconv.shape), const(wuv.shape), const(wq.shape), const(wkv.shape), const(wgl.shape),
            const((CONV_WIDTH, D_CONV)), const((1, D_GMLP)), const((1, D_GMLP)),
            const((GMLP_HEADS, GMLP_CHUNK, GMLP_CHUNK)), const((GMLP_CHUNK, D_GMLP)),
        ],
        out_specs=[row_spec(D_CONV + D_GMLP), col_spec(D_NSA), row_spec(D_KV), row_spec(D_KV),
                   row_spec(2 * D_KV), col_spec(2 * D_KV), col_spec(128)],
        out_shape=[
            jax.ShapeDtypeStruct((rows, D_CONV + D_GMLP), BF16),
            jax.ShapeDtypeStruct((D_NSA, rows), BF16),
            jax.ShapeDtypeStruct((rows, D_KV), BF16),
            jax.ShapeDtypeStruct((rows, D_KV), BF16),
            jax.ShapeDtypeStruct((rows, 2 * D_KV), BF16),
            jax.ShapeDtypeStruct((2 * D_KV, rows), BF16),
            jax.ShapeDtypeStruct((128, rows), F32),
        ],
        compiler_params=pltpu.CompilerParams(
            dimension_semantics=("parallel",), vmem_limit_bytes=VMEM_LIMIT),
        name="in_proj",
    )(x2d, x2d, wconv, wuv, wq, wkv, wgl, conv_w, gln_g.reshape(1, D_GMLP), gln_b.reshape(1, D_GMLP),
      gmlp_w, gb_exp)


def _compress_kernel(k_ref, v_ref, w1k_ref, w1v_ref, pek_ref, pev_ref, b1k_ref, b1v_ref,
                     w2k_ref, w2v_ref, kc_ref, vct_ref):
    def one(x2d, w1, pe2, b1, w2):
        n = x2d.shape[0]
        r = _dot(x2d, w1)
        cst = _dot(pe2, w1)
        outs = []
        for g in range(NSA_KV_HEADS):
            lo = slice(g * CMP_HIDDEN, (g + 1) * CMP_HIDDEN)
            hi = slice((NSA_KV_HEADS + g) * CMP_HIDDEN, (NSA_KV_HEADS + g + 1) * CMP_HIDDEN)
            first = r[:, lo] + cst[0:1, lo]
            second = r[:, hi] + cst[1:2, hi]
            h = _gelu_tanh(first + pltpu.roll(second, n - 1, 0) + b1)
            outs.append(_dot(h.astype(BF16), w2))
        return jnp.concatenate(outs, axis=1)

    kc_ref[0] = one(k_ref[0], w1k_ref[...], pek_ref[...], b1k_ref[...], w2k_ref[...]).astype(BF16)
    vct_ref[0] = one(v_ref[0], w1v_ref[...], pev_ref[...], b1v_ref[...], w2v_ref[...]).T.astype(BF16)


def _expand_w1(w1):
    half = CMP_LEN // 2
    eye = jnp.eye(NSA_KV_HEADS, dtype=w1.dtype)
    w = w1.reshape(2, half, HEAD_DIM, CMP_HIDDEN)
    e = jnp.einsum("aldf,hg->lhdagf", w, eye)
    return e.reshape(half * NSA_KV_HEADS * HEAD_DIM, 2 * NSA_KV_HEADS * CMP_HIDDEN).astype(BF16)


def _expand_pe(pe):
    half = CMP_LEN // 2
    p = pe.reshape(2, half, 1, HEAD_DIM)
    p = jnp.broadcast_to(p, (2, half, NSA_KV_HEADS, HEAD_DIM)).reshape(2, -1)
    return jnp.pad(p, ((0, 6), (0, 0))).astype(BF16)


def _compress(kc2d, vc2d, batch, seq, ck_pe, ck_w1, ck_b1, ck_w2, cv_pe, cv_w1, cv_b1, cv_w2):
    n = seq // CMP_STRIDE
    width = CMP_STRIDE * D_KV
    k3 = kc2d.reshape(batch, n, width)
    v3 = vc2d.reshape(batch, n, width)
    const = lambda shape: pl.BlockSpec(shape, lambda b: (0,) * len(shape))
    blk = pl.BlockSpec((1, n, width), lambda b: (b, 0, 0))
    out = pl.BlockSpec((1, n, D_KV), lambda b: (b, 0, 0))
    out_t = pl.BlockSpec((1, D_KV, n), lambda b: (b, 0, 0))
    wshape = (width, 2 * NSA_KV_HEADS * CMP_HIDDEN)
    return pl.pallas_call(
        _compress_kernel,
        grid=(batch,),
        in_specs=[blk, blk, const(wshape), const(wshape), const((8, width)), const((8, width)),
                  const((1, CMP_HIDDEN)), const((1, CMP_HIDDEN)),
                  const((CMP_HIDDEN, HEAD_DIM)), const((CMP_HIDDEN, HEAD_DIM))],
        out_specs=[out, out_t],
        out_shape=[jax.ShapeDtypeStruct((batch, n, D_KV), BF16),
                   jax.ShapeDtypeStruct((batch, D_KV, n), BF16)],
        compiler_params=pltpu.CompilerParams(
            dimension_semantics=("parallel",), vmem_limit_bytes=VMEM_LIMIT),
        name="compress_kv",
    )(k3, v3, _expand_w1(ck_w1), _expand_w1(cv_w1), _expand_pe(ck_pe), _expand_pe(cv_pe),
      ck_b1.reshape(1, CMP_HIDDEN), cv_b1.reshape(1, CMP_HIDDEN),
      ck_w2.astype(BF16), cv_w2.astype(BF16))


def _pad_group(x, g):
    z = jnp.zeros_like(x)
    return jnp.concatenate([x if gg == g else z for gg in range(NSA_KV_HEADS)], axis=0)


def _cmp_attn_kernel(qt_ref, kc_ref, vct_ref, ovt_ref, ocmpt_ref, selt_ref):
    tq = qt_ref.shape[1]
    n_cmp = kc_ref.shape[1]
    n_blk = ovt_ref.shape[0]
    t0 = pl.program_id(1) * tq
    kc = kc_ref[0]
    vct = vct_ref[0]
    ovt = ovt_ref[...]
    tpos = t0 + lax.broadcasted_iota(jnp.int32, (n_cmp, tq), 1)
    nidx = lax.broadcasted_iota(jnp.int32, (n_cmp, tq), 0)
    visible = nidx * CMP_STRIDE + (CMP_LEN - 1) <= tpos

    jj = lax.broadcasted_iota(jnp.int32, (n_blk, tq), 0)
    cur = jnp.right_shift(t0 + lax.broadcasted_iota(jnp.int32, (n_blk, tq), 1), SEL_SHIFT)
    valid = jj <= cur
    forced = (jj == 0) | (jj == cur) | (jj == cur - 1)
    sub8 = lax.broadcasted_iota(jnp.int32, (8, tq), 0)

    outs = []
    sel_t = []
    for g in range(NSA_KV_HEADS):
        vg = vct[g * HEAD_DIM:(g + 1) * HEAD_DIM]
        psum = jnp.zeros((n_cmp, tq), F32)
        for r in range(NSA_GROUP):
            h = g * NSA_GROUP + r
            qh = _pad_group(qt_ref[h * HEAD_DIM:(h + 1) * HEAD_DIM, :], g)
            s = jnp.where(visible, _dot(kc, qh), NEG)
            e = jnp.exp2(s - jnp.max(s, axis=0, keepdims=True))
            p = jnp.where(visible, e * (1.0 / jnp.sum(e, axis=0, keepdims=True)), 0.0)
            outs.append(_dot(vg, p.astype(BF16)))
            psum = psum + p
        hi = psum.astype(BF16)
        lo = (psum - hi.astype(F32)).astype(BF16)
        imp_t = _dot(ovt, hi) + _dot(ovt, lo)
        score = jnp.where(forced, -NEG, jnp.where(valid, imp_t, NEG))
        groups = [score[8 * b:8 * b + 8] for b in range(n_blk // 8)]
        ranks = [jnp.zeros((8, tq), F32) for _ in groups]
        for i in range(n_blk):
            row = jnp.broadcast_to(score[i:i + 1, :], (8, tq))
            for b, blk in enumerate(groups):
                if 8 * b > i:
                    before = jnp.where(row >= blk, 1.0, 0.0)
                elif 8 * b + 7 < i:
                    before = jnp.where(row > blk, 1.0, 0.0)
                else:
                    before = jnp.where(sub8 + 8 * b > i, jnp.where(row >= blk, 1.0, 0.0),
                                       jnp.where(row > blk, 1.0, 0.0))
                ranks[b] = ranks[b] + before
        rank = jnp.concatenate(ranks, axis=0)
        sel_t.append(jnp.where(rank < float(min(N_SELECT, n_blk)), 1.0, 0.0))
    ocmpt_ref[...] = jnp.concatenate(outs, axis=0)
    selt_ref[...] = jnp.concatenate(sel_t, axis=0).astype(BF16)


def _overlap_t(n_cmp_pad, n_blk):
    st = np.arange(n_cmp_pad) * CMP_STRIDE
    bs = np.arange(n_blk) * SEL_BLOCK
    ov = np.minimum(st[:, None] + CMP_LEN, bs[None, :] + SEL_BLOCK) - np.maximum(st[:, None], bs[None, :])
    return jnp.asarray((np.clip(ov, 0, None) / CMP_LEN).T, dtype=BF16)


def _cmp_attn(qt, kc, vct, batch, seq):
    tq = CMP_Q_TILE
    n_cmp = kc.shape[1]
    n_blk = seq // SEL_BLOCK
    per = seq // tq
    cols = lambda height: pl.BlockSpec((height, tq), lambda b, j: (0, b * per + j))
    return pl.pallas_call(
        _cmp_attn_kernel,
        grid=(batch, per),
        in_specs=[
            cols(D_NSA),
            pl.BlockSpec((1, n_cmp, D_KV), lambda b, j: (b, 0, 0)),
            pl.BlockSpec((1, D_KV, n_cmp), lambda b, j: (b, 0, 0)),
            pl.BlockSpec((n_blk, n_cmp), lambda b, j: (0, 0)),
        ],
        out_specs=[cols(D_NSA), cols(NSA_KV_HEADS * n_blk)],
        out_shape=[jax.ShapeDtypeStruct((D_NSA, batch * seq), F32),
                   jax.ShapeDtypeStruct((NSA_KV_HEADS * n_blk, batch * seq), BF16)],
        compiler_params=pltpu.CompilerParams(
            dimension_semantics=("parallel", "parallel"), vmem_limit_bytes=VMEM_LIMIT),
        name="cmp_attn_select",
    )(qt, kc, vct, _overlap_t(n_cmp, n_blk))


def _sel_win_kernel(qt_ref, kk_ref, vvt_ref, selt_ref, ocmpt_ref, gatet_ref, out_ref,
                    bias_ref, acc_ref, qaug_ref):
    qb = qt_ref.shape[1]
    n_blk = selt_ref.shape[0] // NSA_KV_HEADS
    kc = SEL_KEY_CHUNK
    lanes = NSA_GROUP * qb
    s0 = pl.program_id(1) * qb
    n_full = s0 // kc
    win_keys = WINDOW + qb
    win_start = pl.multiple_of(jnp.maximum(s0 - WINDOW, 0), qb)

    blk_per_chunk = kc // SEL_BLOCK
    bias = jnp.where(selt_ref[...].astype(F32) > 0.5, 0.0, NEG)
    if blk_per_chunk < 8:
        bias_ref[...] = jnp.zeros(bias_ref.shape, F32)
    for c in range(bias_ref.shape[0]):
        bias_ref[c, :blk_per_chunk, :] = bias[c * blk_per_chunk:(c + 1) * blk_per_chunk]
    tq =s0 + lax.broadcasted_iota(jnp.int32, (1, qb), 1)
    dist = tq - (win_start + lax.broadcasted_iota(jnp.int32, (win_keys, qb), 0))
    wbias = jnp.where((dist >= 0) & (dist < WINDOW), 0.0, NEG)
    wbias = jnp.concatenate([wbias] * NSA_GROUP, axis=1)
    gt = gatet_ref[...]

    groups = range(NSA_KV_HEADS)
    qg = [_pad_group(jnp.concatenate(
        [qt_ref[h * HEAD_DIM:(h + 1) * HEAD_DIM, :]
         for h in range(g * NSA_GROUP, (g + 1) * NSA_GROUP)], axis=1), g) for g in groups]
    onehot = jnp.where(
        jnp.right_shift(lax.broadcasted_iota(jnp.int32, (kc, D_KV), 0), SEL_SHIFT)
        == lax.broadcasted_iota(jnp.int32, (kc, D_KV), 1), 1.0, 0.0).astype(BF16)
    for g in groups:
        qaug_ref[g, :D_KV, :] = qg[g]
        qaug_ref[g, D_KV:, :] = jnp.zeros((D_KV, lanes), BF16)

    def step(g, k0, causal, m, l):
        rows = jnp.concatenate([bias_ref[g * (n_blk // blk_per_chunk) + k0 // kc],
                                jnp.zeros((8, qb), F32)], axis=0).astype(BF16)
        qaug_ref[g, D_KV:D_KV + 16, :] = jnp.concatenate([rows] * NSA_GROUP, axis=1)
        keys = jnp.concatenate([kk_ref[pl.ds(k0, kc), :D_KV], onehot], axis=1)
        s = _dot(keys, qaug_ref[g])
        if causal:
            seen = k0 + lax.broadcasted_iota(jnp.int32, (kc, qb), 0) <= tq
            s = jnp.concatenate([jnp.where(seen, s[:, r * qb:(r + 1) * qb], NEG)
                                 for r in range(NSA_GROUP)], axis=1)
        m_new = jnp.maximum(m, jnp.max(s, axis=0, keepdims=True))
        alpha = jnp.exp2(m - m_new)
        p = jnp.exp2(s - m_new)
        v = vvt_ref[g * HEAD_DIM:(g + 1) * HEAD_DIM, pl.ds(k0, kc)]
        acc_ref[g] = alpha * acc_ref[g] + _dot(v, p.astype(BF16))
        return m_new, alpha * l + jnp.sum(p, axis=0, keepdims=True)

    def body(c, carry):
        k0 = pl.multiple_of(c * kc, kc)
        return tuple(step(g, k0, False, *carry[g]) for g in groups)

    acc_ref[...] = jnp.zeros(acc_ref.shape, F32)
    init = (jnp.full((1, lanes), NEG, F32), jnp.zeros((1, lanes), F32))
    carry = lax.fori_loop(0, n_full, body, (init,) * NSA_KV_HEADS)
    k_last = pl.multiple_of(n_full * kc, kc)

    pieces = []
    for g in groups:
        _, l = step(g, k_last, True, *carry[g])
        o_s = acc_ref[g] * (1.0 / l)

        sw = _dot(kk_ref[pl.ds(win_start, win_keys), D_KV:], qg[g]) + wbias
        pw = jnp.exp2(sw - jnp.max(sw, axis=0, keepdims=True))
        vw = vvt_ref[D_KV + g * HEAD_DIM:D_KV + (g + 1) * HEAD_DIM, pl.ds(win_start, win_keys)]
        o_w = _dot(vw, pw.astype(BF16)) * (1.0 / jnp.sum(pw, axis=0, keepdims=True))

        for r in range(NSA_GROUP):
            h = g * NSA_GROUP + r
            ln = slice(r * qb, (r + 1) * qb)
            pieces.append(gt[3 * h:3 * h + 1] * ocmpt_ref[h * HEAD_DIM:(h + 1) * HEAD_DIM, :]
                          + gt[3 * h + 1:3 * h + 2] * o_s[:, ln]
                          + gt[3 * h + 2:3 * h + 3] * o_w[:, ln])
    out_ref[...] = jnp.concatenate(pieces, axis=0).T.astype(BF16)


def _sel_win(qt, kk, vvt, selt, ocmpt, gatet, batch, seq):
    qb = Q_BLOCK
    per = seq // qb
    cols = lambda height: pl.BlockSpec((height, qb), lambda b, i: (0, b * per + i))
    return pl.pallas_call(
        _sel_win_kernel,
        grid=(batch, per),
        in_specs=[cols(D_NSA),
                  pl.BlockSpec((seq, 2 * D_KV), lambda b, i: (b, 0)),
                  pl.BlockSpec((2 * D_KV, seq), lambda b, i: (0, b)),
                  cols(selt.shape[0]), cols(D_NSA), cols(128)],
        out_specs=pl.BlockSpec((qb, D_NSA), lambda b, i: (b * per + i, 0)),
        out_shape=jax.ShapeDtypeStruct((batch * seq, D_NSA), BF16),
        scratch_shapes=[pltpu.VMEM((selt.shape[0] * SEL_BLOCK // SEL_KEY_CHUNK, 8, qb), F32),
                        pltpu.VMEM((NSA_KV_HEADS, HEAD_DIM, NSA_GROUP * qb), F32),
                        pltpu.VMEM((NSA_KV_HEADS, 2 * D_KV, NSA_GROUP * qb), BF16)],
        compiler_params=pltpu.CompilerParams(
            dimension_semantics=("parallel", "arbitrary"), vmem_limit_bytes=VMEM_LIMIT),
        name="sel_win_attn",
    )(qt, kk, vvt, selt, ocmpt, gatet)


def _out_ln_kernel(x_ref, yab_ref, yc_ref, wab_ref, wc_ref, g_ref, b_ref, o_ref):
    mix = _dot(yab_ref[...], wab_ref[...]) + _dot(yc_ref[...], wc_ref[...])
    o_ref[...] = _layer_norm(ALPHA * x_ref[...] + mix, g_ref[...], b_ref[...])


def _out_ln(x2d, yab, yc, w_out, g, b):
    rows = x2d.shape[0]
    tm = ROW_TILE
    w = w_out.astype(BF16)
    d_ab = D_CONV + D_GMLP
    row = lambda width: pl.BlockSpec((tm, width), lambda i: (i, 0))
    const = lambda shape: pl.BlockSpec(shape, lambda i: (0,) * len(shape))
    return pl.pallas_call(
        _out_ln_kernel,
        grid=(rows // tm,),
        in_specs=[row(D_MODEL), row(d_ab), row(D_NSA), const((d_ab, D_MODEL)), const((D_NSA, D_MODEL)),
                  const((1, D_MODEL)), const((1, D_MODEL))],
        out_specs=row(D_MODEL),
        out_shape=jax.ShapeDtypeStruct((rows, D_MODEL), F32),
        compiler_params=pltpu.CompilerParams(
            dimension_semantics=("parallel",), vmem_limit_bytes=VMEM_LIMIT),
        name="out_proj_ln",
    )(x2d, yab, yc, w[:d_ab], w[d_ab:], g.reshape(1, D_MODEL), b.reshape(1, D_MODEL))


def kernel(x, ffn1_gate, ffn1_up, ffn1_down, ln1_g, ln1_b, w_in, conv_w, gmlp_ln_g, gmlp_ln_b, gmlp_w, gmlp_b, ck_pe, ck_w1, ck_b1, ck_w2, cv_pe, cv_w1, cv_b1, cv_w2, w_out, ln2_g, ln2_b, ffn2_gate, ffn2_up, ffn2_down, ln3_g, ln3_b):
    batch, seq, d = x.shape
    assert d == D_MODEL and seq % max(ROW_TILE, CMP_Q_TILE, SEL_KEY_CHUNK) == 0
    h = x.reshape(batch * seq, d)
    for l in range(ffn1_gate.shape[0]):
        h = _ffn_ln(h, ffn1_gate[l], ffn1_up[l], ffn1_down[l], ln1_g[l], ln1_b[l])
        yab, qt, kcr, vcr, kk, vvt, gatet = _proj(h, seq, w_in[l], conv_w[l], gmlp_ln_g[l],
                                                  gmlp_ln_b[l], gmlp_w[l], gmlp_b[l])
        kc, vct = _compress(kcr, vcr, batch, seq, ck_pe[l], ck_w1[l], ck_b1[l], ck_w2[l],
                            cv_pe[l], cv_w1[l], cv_b1[l], cv_w2[l])
        ocmpt, selt = _cmp_attn(qt, kc, vct, batch, seq)
        yc = _sel_win(qt, kk, vvt, selt, ocmpt, gatet, batch, seq)
        h = _out_ln(h, yab, yc, w_out[l], ln2_g[l], ln2_b[l])
        h = _ffn_ln(h, ffn2_gate[l], ffn2_up[l], ffn2_down[l], ln3_g[l], ln3_b[l])
    return h.reshape(batch, seq, d)
```

```python
import functools

import numpy as np
import jax
import jax.numpy as jnp
from jax import lax
from jax.experimental import pallas as pl
from jax.experimental.pallas import tpu as pltpu

F32 = jnp.float32
BF16 = jnp.bfloat16

D_MODEL = 1024
D_FF = 2816
D_CONV = 256
CONV_WIDTH = 3
D_GMLP = 256
GMLP_HEADS = 4
GMLP_HEAD_DIM = D_GMLP // GMLP_HEADS
GMLP_CHUNK = 128
HEAD_DIM = 64
NSA_HEADS = 8
NSA_KV_HEADS = 2
NSA_GROUP = NSA_HEADS // NSA_KV_HEADS
D_NSA = NSA_HEADS * HEAD_DIM
D_KV = NSA_KV_HEADS * HEAD_DIM
CMP_LEN = 32
CMP_STRIDE = 16
CMP_HIDDEN = 128
SEL_BLOCK = 64
SEL_SHIFT = 6
N_SELECT = 16
WINDOW = 512
DEPTH = 2
ALPHA = (2 * DEPTH) ** 0.25
LN_EPS = 1e-5
NEG = -1e30
LOG2E = 1.4426950408889634
Q_SCALE = HEAD_DIM ** -0.5 * LOG2E

FF_CHUNK = 256
ROW_TILE = 512
CMP_Q_TILE = 512
Q_BLOCK = 128
SEL_KEY_CHUNK = 512
VMEM_LIMIT_FFN = 56 * 1024 * 1024
VMEM_LIMIT = 48 * 1024 * 1024


def _dot(a, b):
    return jnp.dot(a, b, preferred_element_type=F32)


def _dot_nt(a, b):
    return lax.dot_general(a, b, (((1,), (1,)), ((), ())), preferred_element_type=F32)


def _layer_norm(y, g, b):
    mu = jnp.mean(y, axis=-1, keepdims=True)
    d = y - mu
    var = jnp.mean(d * d, axis=-1, keepdims=True)
    return d * lax.rsqrt(var + LN_EPS) * g + b


def _gelu_tanh(x):
    return 0.5 * x * (1.0 + jnp.tanh(0.7978845608028654 * (x + 0.044715 * (x * x * x))))


def _ffn_ln_kernel(with_mix, x_ref, *refs):
    x = x_ref[...]
    if with_mix:
        yab_ref, yc_ref, wab_ref, wc_ref, g2_ref, b2_ref, *refs = refs
        mix = _dot(yab_ref[...], wab_ref[...]) + _dot(yc_ref[...], wc_ref[...])
        x = _layer_norm(ALPHA * x + mix, g2_ref[...], b2_ref[...])
    wg_ref, wu_ref, wd_ref, g_ref, b_ref, o_ref, acc_ref = refs
    xb = x.astype(BF16)
    for c in range(D_FF // FF_CHUNK):
        cols = slice(c * FF_CHUNK, (c + 1) * FF_CHUNK)
        gate = _dot(xb, wg_ref[:, cols])
        up = _dot(xb, wu_ref[:, cols])
        h = (gate * jax.nn.sigmoid(gate) * up).astype(BF16)
        down = _dot(h, wd_ref[cols, :])
        if c == 0:
            acc_ref[...] = down
        else:
            acc_ref[...] += down
    y = ALPHA * x + 0.5 * acc_ref[...]
    o_ref[...] = _layer_norm(y, g_ref[...], b_ref[...])


def _ffn_ln(x2d, wg, wu, wd, g, b, mix=None):
    rows = x2d.shape[0]
    resident = functools.partial(pl.BlockSpec, pipeline_mode=pl.Buffered(1))
    row = lambda width: pl.BlockSpec((ROW_TILE, width), lambda i: (i, 0))
    vec = pl.BlockSpec((1, D_MODEL), lambda i: (0, 0))
    operands, specs = [x2d], [row(D_MODEL)]
    if mix is not None:
        yab, yc, w_out, g2, b2 = mix
        d_ab = yab.shape[1]
        w = w_out.astype(BF16)
        operands += [yab, yc, w[:d_ab], w[d_ab:], g2.reshape(1, D_MODEL), b2.reshape(1, D_MODEL)]
        specs += [row(d_ab), row(yc.shape[1]), resident((d_ab, D_MODEL), lambda i: (0, 0)),
                  resident((yc.shape[1], D_MODEL), lambda i: (0, 0)), vec, vec]
    operands += [wg.astype(BF16), wu.astype(BF16), wd.astype(BF16),
                 g.reshape(1, D_MODEL), b.reshape(1, D_MODEL)]
    specs += [resident((D_MODEL, D_FF), lambda i: (0, 0)), resident((D_MODEL, D_FF), lambda i: (0, 0)),
              resident((D_FF, D_MODEL), lambda i: (0, 0)), vec, vec]
    return pl.pallas_call(
        functools.partial(_ffn_ln_kernel, mix is not None),
        grid=(rows // ROW_TILE,),
        in_specs=specs,
        out_specs=row(D_MODEL),
        out_shape=jax.ShapeDtypeStruct((rows, D_MODEL), F32),
        scratch_shapes=[pltpu.VMEM((ROW_TILE, D_MODEL), F32)],
        compiler_params=pltpu.CompilerParams(
            dimension_semantics=("parallel",), vmem_limit_bytes=VMEM_LIMIT_FFN),
        name="mix_ffn_ln" if mix is not None else "ffn_ln",
    )(*operands)


def _proj_kernel(tiles_per_seq, x_ref, xh_ref, wconv_ref, wuv_ref, wq_ref, wkv_ref, wgl_ref,
                 convw_ref, glg_ref, glb_ref, gw_ref, gb_ref,
                 yab_ref, qt_ref, kc_ref, vc_ref, kk_ref, vvt_ref, gatet_ref):
    i = pl.program_id(0)
    tm = x_ref.shape[0]
    xb = x_ref[...].astype(BF16)

    hbc = _dot(xb, wconv_ref[...])
    z = hbc[:, 2 * D_CONV:] * hbc[:, :D_CONV]
    hbc_halo = _dot(xh_ref[...].astype(BF16), wconv_ref[...])
    z_halo = hbc_halo[:, 2 * D_CONV:] * hbc_halo[:, :D_CONV]
    z_halo = jnp.where(i % tiles_per_seq == 0, 0.0, z_halo)
    row = lax.broadcasted_iota(jnp.int32, (tm, D_CONV), 0)
    z1 = jnp.where(row == 0, z_halo[7:8], pltpu.roll(z, 1, 0))
    z2 = jnp.where(row == 0, z_halo[6:7], jnp.where(row == 1, z_halo[7:8], pltpu.roll(z, 2, 0)))
    cw = convw_ref[...]
    y_a = hbc[:, D_CONV:2 * D_CONV] * (cw[0:1] * z2 + cw[1:2] * z1 + cw[2:3] * z)

    zz = _gelu_tanh(_dot(xb, wuv_ref[...]))
    u = zz[:, :D_GMLP]
    v = _layer_norm(zz[:, D_GMLP:], glg_ref[...], glb_ref[...]).astype(BF16)
    pr = lax.broadcasted_iota(jnp.int32, (GMLP_CHUNK, GMLP_CHUNK), 0)
    pc = lax.broadcasted_iota(jnp.int32, (GMLP_CHUNK, GMLP_CHUNK), 1)
    w_tril = [jnp.where(pr >= pc, gw_ref[h], 0.0).astype(BF16) for h in range(GMLP_HEADS)]
    sv_chunks = []
    for c in range(tm // GMLP_CHUNK):
        vch = v[c * GMLP_CHUNK:(c + 1) * GMLP_CHUNK]
        parts = [_dot(w_tril[h], vch[:, h * GMLP_HEAD_DIM:(h + 1) * GMLP_HEAD_DIM])
                 for h in range(GMLP_HEADS)]
        sv_chunks.append(jnp.concatenate(parts, axis=1) + gb_ref[...])
    y_b = u * jnp.concatenate(sv_chunks, axis=0)

    yab_ref[...] = jnp.concatenate([y_a, y_b], axis=1).astype(BF16)
    qt_ref[...] = (_dot(xb, wq_ref[...]) * Q_SCALE).T.astype(BF16)
    kv = _dot(xb, wkv_ref[...])
    kc_ref[...] = kv[:, :D_KV].astype(BF16)
    vc_ref[...] = kv[:, D_KV:2 * D_KV].astype(BF16)
    kk_ref[...] = jnp.concatenate([kv[:, 2 * D_KV:3 * D_KV], kv[:, 4 * D_KV:5 * D_KV]], axis=1).astype(BF16)
    vvt_ref[...] = jnp.concatenate([kv[:, 3 * D_KV:4 * D_KV], kv[:, 5 * D_KV:]], axis=1).T.astype(BF16)
    gatet_ref[...] = jax.nn.sigmoid(_dot(xb, wgl_ref[...])).T


def _proj(x2d, seq, w_in, conv_w, gln_g, gln_b, gmlp_w, gmlp_b):
    rows = x2d.shape[0]
    tm = ROW_TILE
    w = w_in.astype(BF16)
    o = 0
    wconv = w[:, o:o + 3 * D_CONV]; o += 3 * D_CONV
    wuv = w[:, o:o + 2 * D_GMLP]; o += 2 * D_GMLP
    wq = w[:, o:o + D_NSA]; o += D_NSA
    wkv = w[:, o:o + 6 * D_KV]; o += 6 * D_KV
    n_gate = 3 * NSA_HEADS
    wgl = jnp.pad(w[:, o:o + n_gate], ((0, 0), (0, 128 - n_gate)))
    gb_exp = jnp.repeat(gmlp_b.T, GMLP_HEAD_DIM, axis=1)
    const = lambda shape: pl.BlockSpec(shape, lambda i: (0,) * len(shape))
    row_spec = lambda width: pl.BlockSpec((tm, width), lambda i: (i, 0))
    col_spec = lambda height: pl.BlockSpec((height, tm), lambda i: (0, i))
    return pl.pallas_call(
        functools.partial(_proj_kernel, seq // tm),
        grid=(rows // tm,),
        in_specs=[
            row_spec(D_MODEL),
            pl.BlockSpec((8, D_MODEL), lambda i: (jnp.maximum(i * (tm // 8) - 1, 0), 0)),
            const(wconv.shape), const(wuv.shape), const(wq.shape), const(wkv.shape), const(wgl.shape),
            const((CONV_WIDTH, D_CONV)), const((1, D_GMLP)), const((1, D_GMLP)),
            const((GMLP_HEADS, GMLP_CHUNK, GMLP_CHUNK)), const((GMLP_CHUNK, D_GMLP)),
        ],
        out_specs=[row_spec(D_CONV + D_GMLP), col_spec(D_NSA), row_spec(D_KV), row_spec(D_KV),
                   row_spec(2 * D_KV), col_spec(2 * D_KV), col_spec(128)],
        out_shape=[
            jax.ShapeDtypeStruct((rows, D_CONV + D_GMLP), BF16),
            jax.ShapeDtypeStruct((D_NSA, rows), BF16),
            jax.ShapeDtypeStruct((rows, D_KV), BF16),
            jax.ShapeDtypeStruct((rows, D_KV), BF16),
            jax.ShapeDtypeStruct((rows, 2 * D_KV), BF16),
            jax.ShapeDtypeStruct((2 * D_KV, rows), BF16),
            jax.ShapeDtypeStruct((128, rows), F32),
        ],
        compiler_params=pltpu.CompilerParams(
            dimension_semantics=("parallel",), vmem_limit_bytes=VMEM_LIMIT),
        name="in_proj",
    )(x2d, x2d, wconv, wuv, wq, wkv, wgl, conv_w, gln_g.reshape(1, D_GMLP), gln_b.reshape(1, D_GMLP),
      gmlp_w, gb_exp)


def _compress_kernel(k_ref, v_ref, w1k_ref, w1v_ref, pek_ref, pev_ref, b1k_ref, b1v_ref,
                     w2k_ref, w2v_ref, kc_ref, vct_ref):
    def one(x2d, w1, pe2, b1, w2):
        n = x2d.shape[0]
        r = _dot(x2d, w1)
        cst = _dot(pe2, w1)
        outs = []
        for g in range(NSA_KV_HEADS):
            lo = slice(g * CMP_HIDDEN, (g + 1) * CMP_HIDDEN)
            hi = slice((NSA_KV_HEADS + g) * CMP_HIDDEN, (NSA_KV_HEADS + g + 1) * CMP_HIDDEN)
            first = r[:, lo] + cst[0:1, lo]
            second = r[:, hi] + cst[1:2, hi]
            h = _gelu_tanh(first + pltpu.roll(second, n - 1, 0) + b1)
            outs.append(_dot(h.astype(BF16), w2))
        return jnp.concatenate(outs, axis=1)

    kc_ref[0] = one(k_ref[0], w1k_ref[...], pek_ref[...], b1k_ref[...], w2k_ref[...]).astype(BF16)
    vct_ref[0] = one(v_ref[0], w1v_ref[...], pev_ref[...], b1v_ref[...], w2v_ref[...]).T.astype(BF16)


def _expand_w1(w1):
    half = CMP_LEN // 2
    eye = jnp.eye(NSA_KV_HEADS, dtype=w1.dtype)
    w = w1.reshape(2, half, HEAD_DIM, CMP_HIDDEN)
    e = jnp.einsum("aldf,hg->lhdagf", w, eye)
    return e.reshape(half * NSA_KV_HEADS * HEAD_DIM, 2 * NSA_KV_HEADS * CMP_HIDDEN).astype(BF16)


def _expand_pe(pe):
    half = CMP_LEN // 2
    p = pe.reshape(2, half, 1, HEAD_DIM)
    p = jnp.broadcast_to(p, (2, half, NSA_KV_HEADS, HEAD_DIM)).reshape(2, -1)
    return jnp.pad(p, ((0, 6), (0, 0))).astype(BF16)


def _compress(kc2d, vc2d, batch, seq, ck_pe, ck_w1, ck_b1, ck_w2, cv_pe, cv_w1, cv_b1, cv_w2):
    n = seq // CMP_STRIDE
    width = CMP_STRIDE * D_KV
    k3 = kc2d.reshape(batch, n, width)
    v3 = vc2d.reshape(batch, n, width)
    const = lambda shape: pl.BlockSpec(shape, lambda b: (0,) * len(shape))
    blk = pl.BlockSpec((1, n, width), lambda b: (b, 0, 0))
    out = pl.BlockSpec((1, n, D_KV), lambda b: (b, 0, 0))
    out_t = pl.BlockSpec((1, D_KV, n), lambda b: (b, 0, 0))
    wshape = (width, 2 * NSA_KV_HEADS * CMP_HIDDEN)
    return pl.pallas_call(
        _compress_kernel,
        grid=(batch,),
        in_specs=[blk, blk, const(wshape), const(wshape), const((8, width)), const((8, width)),
                  const((1, CMP_HIDDEN)), const((1, CMP_HIDDEN)),
                  const((CMP_HIDDEN, HEAD_DIM)), const((CMP_HIDDEN, HEAD_DIM))],
        out_specs=[out, out_t],
        out_shape=[jax.ShapeDtypeStruct((batch, n, D_KV), BF16),
                   jax.ShapeDtypeStruct((batch, D_KV, n), BF16)],
        compiler_params=pltpu.CompilerParams(
            dimension_semantics=("parallel",), vmem_limit_bytes=VMEM_LIMIT),
        name="compress_kv",
    )(k3, v3, _expand_w1(ck_w1), _expand_w1(cv_w1), _expand_pe(ck_pe), _expand_pe(cv_pe),
      ck_b1.reshape(1, CMP_HIDDEN), cv_b1.reshape(1, CMP_HIDDEN),
      ck_w2.astype(BF16), cv_w2.astype(BF16))


def _pad_group(x, g):
    z = jnp.zeros_like(x)
    return jnp.concatenate([x if gg == g else z for gg in range(NSA_KV_HEADS)], axis=0)


def _cmp_attn_kernel(qt_ref, kc_ref, vct_ref, ovt_ref, ocmpt_ref, selt_ref, score_ref, rank_ref):
    tq = qt_ref.shape[1]
    n_cmp = kc_ref.shape[1]
    n_blk = ovt_ref.shape[0]
    t0 = pl.program_id(1) * tq
    kc = kc_ref[0]
    vct = vct_ref[0]
    ovt = ovt_ref[...]
    tpos = t0 + lax.broadcasted_iota(jnp.int32, (n_cmp, tq), 1)
    nidx = lax.broadcasted_iota(jnp.int32, (n_cmp, tq), 0)
    visible = nidx * CMP_STRIDE + (CMP_LEN - 1) <= tpos

    jj = lax.broadcasted_iota(jnp.int32, (n_blk, tq), 0)
    cur = jnp.right_shift(t0 + lax.broadcasted_iota(jnp.int32, (n_blk, tq), 1), SEL_SHIFT)
    valid = jj <= cur
    forced = (jj == 0) | (jj == cur) | (jj == cur - 1)
    sub8 = lax.broadcasted_iota(jnp.int32, (8, tq), 0)

    outs = []
    for g in range(NSA_KV_HEADS):
        vg = vct[g * HEAD_DIM:(g + 1) * HEAD_DIM]
        psum = jnp.zeros((n_cmp, tq), F32)
        for r in range(NSA_GROUP):
            h = g * NSA_GROUP + r
            qh = _pad_group(qt_ref[h * HEAD_DIM:(h + 1) * HEAD_DIM, :], g)
            s = jnp.where(visible, _dot(kc, qh), NEG)
            e = jnp.exp2(s - jnp.max(s, axis=0, keepdims=True))
            p = jnp.where(visible, e * (1.0 / jnp.sum(e, axis=0, keepdims=True)), 0.0)
            outs.append(_dot(vg, p.astype(BF16)))
            psum = psum + p
        hi = psum.astype(BF16)
        lo = (psum - hi.astype(F32)).astype(BF16)
        imp_t = _dot(ovt, hi) + _dot(ovt, lo)
        score_ref[g] = jnp.where(forced, -NEG, jnp.where(valid, imp_t, NEG))
    ocmpt_ref[...] = jnp.concatenate(outs, axis=0)

    n_grp = n_blk // 8
    rank_ref[...] = jnp.zeros(rank_ref.shape, F32)
    last_grp = jnp.right_shift(t0 + tq - 1, SEL_SHIFT + 3)

    def count(g, gi, gj):
        rows, blk = score_ref[g, 8 * gi:8 * gi + 8], score_ref[g, 8 * gj:8 * gj + 8]
        total = rank_ref[g, 8 * gj:8 * gj + 8]
        for k in range(8):
            row = jnp.broadcast_to(rows[k:k + 1, :], (8, tq))
            if gi < gj:
                before = jnp.where(row >= blk, 1.0, 0.0)
            elif gi > gj:
                before = jnp.where(row > blk, 1.0, 0.0)
            else:
                before = jnp.where(sub8 > k, jnp.where(row >= blk, 1.0, 0.0),
                                   jnp.where(row > blk, 1.0, 0.0))
            total = total + before
        rank_ref[g, 8 * gj:8 * gj + 8] = total

    for hi_grp in range(n_grp):
        @pl.when(last_grp >= hi_grp)
        def _(hi_grp=hi_grp):
            for g in range(NSA_KV_HEADS):
                for lo_grp in range(hi_grp):
                    count(g, hi_grp, lo_grp)
                    count(g, lo_grp, hi_grp)
                count(g, hi_grp, hi_grp)

    for g in range(NSA_KV_HEADS):
        selt_ref[g * n_blk:(g + 1) * n_blk, :] = jnp.where(
            rank_ref[g] < float(min(N_SELECT, n_blk)), 1.0, 0.0).astype(BF16)


def _overlap_t(n_cmp_pad, n_blk):
    st = np.arange(n_cmp_pad) * CMP_STRIDE
    bs = np.arange(n_blk) * SEL_BLOCK
    ov = np.minimum(st[:, None] + CMP_LEN, bs[None, :] + SEL_BLOCK) - np.maximum(st[:, None], bs[None, :])
    return jnp.asarray((np.clip(ov, 0, None) / CMP_LEN).T, dtype=BF16)


def _cmp_attn(qt, kc, vct, batch, seq):
    tq = CMP_Q_TILE
    n_cmp = kc.shape[1]
    n_blk = seq // SEL_BLOCK
    per = seq // tq
    cols = lambda height: pl.BlockSpec((height, tq), lambda b, j: (0, b * per + j))
    return pl.pallas_call(
        _cmp_attn_kernel,
        grid=(batch, per),
        in_specs=[
            cols(D_NSA),
            pl.BlockSpec((1, n_cmp, D_KV), lambda b, j: (b, 0, 0)),
            pl.BlockSpec((1, D_KV, n_cmp), lambda b, j: (b, 0, 0)),
            pl.BlockSpec((n_blk, n_cmp), lambda b, j: (0, 0)),
        ],
        out_specs=[cols(D_NSA), cols(NSA_KV_HEADS * n_blk)],
        out_shape=[jax.ShapeDtypeStruct((D_NSA, batch * seq), F32),
                   jax.ShapeDtypeStruct((NSA_KV_HEADS * n_blk, batch * seq), BF16)],
        scratch_shapes=[pltpu.VMEM((NSA_KV_HEADS, n_blk, tq), F32)] * 2,
        compiler_params=pltpu.CompilerParams(
            dimension_semantics=("parallel", "parallel"), vmem_limit_bytes=VMEM_LIMIT),
        name="cmp_attn_select",
    )(qt, kc, vct, _overlap_t(n_cmp, n_blk))


def _sel_win_kernel(qt_ref, kk_ref, vvt_ref, selt_ref, ocmpt_ref, gatet_ref, out_ref,
                    bias_ref, acc_ref, qaug_ref, sa_ref, sb_ref):
    qb = qt_ref.shape[1]
    n_blk = selt_ref.shape[0] // NSA_KV_HEADS
    kc = SEL_KEY_CHUNK
    lanes = NSA_GROUP * qb
    s0 = pl.program_id(1) * qb
    n_full = s0 // kc
    win_keys = WINDOW + qb
    win_start = pl.multiple_of(jnp.maximum(s0 - WINDOW, 0), qb)

    blk_per_chunk = kc // SEL_BLOCK
    bias = jnp.where(selt_ref[...].astype(F32) > 0.5, 0.0, NEG)
    if blk_per_chunk < 8:
        bias_ref[...] = jnp.zeros(bias_ref.shape, F32)
    for c in range(bias_ref.shape[0]):
        bias_ref[c, :blk_per_chunk, :] = bias[c * blk_per_chunk:(c + 1) * blk_per_chunk]
    tq =s0 + lax.broadcasted_iota(jnp.int32, (1, qb), 1)
    dist = tq - (win_start + lax.broadcasted_iota(jnp.int32, (win_keys, qb), 0))
    wbias = jnp.where((dist >= 0) & (dist < WINDOW), 0.0, NEG)
    wbias = jnp.concatenate([wbias] * NSA_GROUP, axis=1)
    gt = gatet_ref[...]

    groups = range(NSA_KV_HEADS)
    qg = [_pad_group(jnp.concatenate(
        [qt_ref[h * HEAD_DIM:(h + 1) * HEAD_DIM, :]
         for h in range(g * NSA_GROUP, (g + 1) * NSA_GROUP)], axis=1), g) for g in groups]
    onehot = jnp.where(
        jnp.right_shift(lax.broadcasted_iota(jnp.int32, (kc, D_KV), 0), SEL_SHIFT)
        == lax.broadcasted_iota(jnp.int32, (kc, D_KV), 1), 1.0, 0.0).astype(BF16)
    for g in groups:
        qaug_ref[g, :D_KV, :] = qg[g]
        qaug_ref[g, D_KV:, :] = jnp.zeros((D_KV, lanes), BF16)

    def scores(g, k0):
        rows = jnp.concatenate([bias_ref[g * (n_blk // blk_per_chunk) + k0 // kc],
                                jnp.zeros((8, qb), F32)], axis=0).astype(BF16)
        qaug_ref[g, D_KV:D_KV + 16, :] = jnp.concatenate([rows] * NSA_GROUP, axis=1)
        if not isinstance(k0, int):
            k0 = pl.multiple_of(k0, kc)
        keys = jnp.concatenate([kk_ref[pl.ds(k0, kc), :D_KV], onehot], axis=1)
        return _dot(keys, qaug_ref[g])

    def update(g, k0, s, m, l):
        m_new = jnp.maximum(m, jnp.max(s, axis=0, keepdims=True))
        alpha = jnp.exp2(m - m_new)
        p = jnp.exp2(s - m_new)
        v = vvt_ref[g * HEAD_DIM:(g + 1) * HEAD_DIM, pl.ds(k0, kc)]
        acc_ref[g] = alpha * acc_ref[g] + _dot(v, p.astype(BF16))
        return m_new, alpha * l + jnp.sum(p, axis=0, keepdims=True)

    def score_into(ref, c):
        for g in groups:
            ref[g] = scores(g, c * kc)

    def softmax_from(ref, c, carry):
        k0 = pl.multiple_of(c * kc, kc)
        return tuple(update(g, k0, ref[g], *carry[g]) for g in groups)

    acc_ref[...] = jnp.zeros(acc_ref.shape, F32)
    init = ((jnp.full((1, lanes), NEG, F32), jnp.zeros((1, lanes), F32)),) * NSA_KV_HEADS
    odd = lax.rem(n_full, 2)

    def peel_odd():
        score_into(sb_ref, 0)
        score_into(sa_ref, 1)
        return softmax_from(sb_ref, 0, init)

    def peel_even():
        score_into(sa_ref, 0)
        return init

    carry = lax.cond(odd == 1, peel_odd, peel_even)

    def body(i, carry):
        c = 2 * i + odd
        score_into(sb_ref, c + 1)
        carry = softmax_from(sa_ref, c, carry)
        score_into(sa_ref, c + 2)
        return softmax_from(sb_ref, c + 1, carry)

    carry = lax.fori_loop(0, n_full // 2, body, carry)
    k_last = pl.multiple_of(n_full * kc, kc)
    seen = k_last + lax.broadcasted_iota(jnp.int32, (kc, qb), 0) <= tq

    def last_scores(g):
        s = sa_ref[g]
        return jnp.concatenate([jnp.where(seen, s[:, r * qb:(r + 1) * qb], NEG)
                                for r in range(NSA_GROUP)], axis=1)

    def window_scores(g):
        return _dot(kk_ref[pl.ds(win_start, win_keys), D_KV:], qg[g]) + wbias

    def window_out(g, sw):
        pw = jnp.exp2(sw - jnp.max(sw, axis=0, keepdims=True))
        vw = vvt_ref[D_KV + g * HEAD_DIM:D_KV + (g + 1) * HEAD_DIM, pl.ds(win_start, win_keys)]
        return _dot(vw, pw.astype(BF16)) * (1.0 / jnp.sum(pw, axis=0, keepdims=True))

    o_sel, o_win, s_win = [], [], [None] * NSA_KV_HEADS
    for g in groups:
        s_win[g] = window_scores(g)
        _, l = update(g, k_last, last_scores(g), *carry[g])
        o_sel.append(acc_ref[g] * (1.0 / l))
    for g in groups:
        o_win.append(window_out(g, s_win[g]))

    pieces = []
    for g in groups:
        o_s, o_w = o_sel[g], o_win[g]
        for r in range(NSA_GROUP):
            h = g * NSA_GROUP + r
            ln = slice(r * qb, (r + 1) * qb)
            pieces.append(gt[3 * h:3 * h + 1] * ocmpt_ref[h * HEAD_DIM:(h + 1) * HEAD_DIM, :]
                          + gt[3 * h + 1:3 * h + 2] * o_s[:, ln]
                          + gt[3 * h + 2:3 * h + 3] * o_w[:, ln])
    out_ref[...] = jnp.concatenate(pieces, axis=0).T.astype(BF16)


def _sel_win(qt, kk, vvt, selt, ocmpt, gatet, batch, seq):
    qb = Q_BLOCK
    per = seq // qb
    cols = lambda height: pl.BlockSpec((height, qb), lambda b, i: (0, b * per + i))
    return pl.pallas_call(
        _sel_win_kernel,
        grid=(batch, per),
        in_specs=[cols(D_NSA),
                  pl.BlockSpec((seq, 2 * D_KV), lambda b, i: (b, 0)),
                  pl.BlockSpec((2 * D_KV, seq), lambda b, i: (0, b)),
                  cols(selt.shape[0]), cols(D_NSA), cols(128)],
        out_specs=pl.BlockSpec((qb, D_NSA), lambda b, i: (b * per + i, 0)),
        out_shape=jax.ShapeDtypeStruct((batch * seq, D_NSA), BF16),
        scratch_shapes=[pltpu.VMEM((selt.shape[0] * SEL_BLOCK // SEL_KEY_CHUNK, 8, qb), F32),
                        pltpu.VMEM((NSA_KV_HEADS, HEAD_DIM, NSA_GROUP * qb), F32),
                        pltpu.VMEM((NSA_KV_HEADS, 2 * D_KV, NSA_GROUP * qb), BF16),
                        pltpu.VMEM((NSA_KV_HEADS, SEL_KEY_CHUNK, NSA_GROUP * qb), F32),
                        pltpu.VMEM((NSA_KV_HEADS, SEL_KEY_CHUNK, NSA_GROUP * qb), F32)],
        compiler_params=pltpu.CompilerParams(
            dimension_semantics=("parallel", "arbitrary"), vmem_limit_bytes=VMEM_LIMIT),
        name="sel_win_attn",
    )(qt, kk, vvt, selt, ocmpt, gatet)


def kernel(x, ffn1_gate, ffn1_up, ffn1_down, ln1_g, ln1_b, w_in, conv_w, gmlp_ln_g, gmlp_ln_b, gmlp_w, gmlp_b, ck_pe, ck_w1, ck_b1, ck_w2, cv_pe, cv_w1, cv_b1, cv_w2, w_out, ln2_g, ln2_b, ffn2_gate, ffn2_up, ffn2_down, ln3_g, ln3_b):
    batch, seq, d = x.shape
    assert d == D_MODEL and seq % max(ROW_TILE, CMP_Q_TILE, SEL_KEY_CHUNK) == 0
    h = x.reshape(batch * seq, d)
    for l in range(ffn1_gate.shape[0]):
        h = _ffn_ln(h, ffn1_gate[l], ffn1_up[l], ffn1_down[l], ln1_g[l], ln1_b[l])
        yab, qt, kcr, vcr, kk, vvt, gatet = _proj(h, seq, w_in[l], conv_w[l], gmlp_ln_g[l],
                                                  gmlp_ln_b[l], gmlp_w[l], gmlp_b[l])
        kc, vct = _compress(kcr, vcr, batch, seq, ck_pe[l], ck_w1[l], ck_b1[l], ck_w2[l],
                            cv_pe[l], cv_w1[l], cv_b1[l], cv_w2[l])
        ocmpt, selt = _cmp_attn(qt, kc, vct, batch, seq)
        yc = _sel_win(qt, kk, vvt, selt, ocmpt, gatet, batch, seq)
        h = _ffn_ln(h, ffn2_gate[l], ffn2_up[l], ffn2_down[l], ln3_g[l], ln3_b[l],
                    mix=(yab, yc, w_out[l], ln2_g[l], ln2_b[l]))
    return h.reshape(batch, seq, d)
```

```python
import functools

import numpy as np
import jax
import jax.numpy as jnp
from jax import lax
from jax.experimental import pallas as pl
from jax.experimental.pallas import tpu as pltpu

F32 = jnp.float32
BF16 = jnp.bfloat16

D_MODEL = 1024
D_FF = 2816
D_CONV = 256
CONV_WIDTH = 3
D_GMLP = 256
GMLP_HEADS = 4
GMLP_HEAD_DIM = D_GMLP // GMLP_HEADS
GMLP_CHUNK = 128
HEAD_DIM = 64
NSA_HEADS = 8
NSA_KV_HEADS = 2
NSA_GROUP = NSA_HEADS // NSA_KV_HEADS
D_NSA = NSA_HEADS * HEAD_DIM
D_KV = NSA_KV_HEADS * HEAD_DIM
CMP_LEN = 32
CMP_STRIDE = 16
CMP_HIDDEN = 128
SEL_BLOCK = 64
SEL_SHIFT = 6
N_SELECT = 16
WINDOW = 512
DEPTH = 2
ALPHA = (2 * DEPTH) ** 0.25
LN_EPS = 1e-5
NEG = -1e30
LOG2E = 1.4426950408889634
Q_SCALE = HEAD_DIM ** -0.5 * LOG2E

FF_CHUNK = 256
ROW_TILE = 512
FFN_ROW_TILE = 512
CMP_Q_TILE = 512
Q_BLOCK = 256
SEL_KEY_CHUNK = 512
VMEM_LIMIT_FFN = 56 * 1024 * 1024
VMEM_LIMIT = 48 * 1024 * 1024


def _dot(a, b):
    return jnp.dot(a, b, preferred_element_type=F32)


def _dot_nt(a, b):
    return lax.dot_general(a, b, (((1,), (1,)), ((), ())), preferred_element_type=F32)


def _layer_norm(y, g, b):
    mu = jnp.mean(y, axis=-1, keepdims=True)
    d = y - mu
    var = jnp.mean(d * d, axis=-1, keepdims=True)
    return d * lax.rsqrt(var + LN_EPS) * g + b


def _gelu_tanh(x):
    return 0.5 * x * (1.0 + jnp.tanh(0.7978845608028654 * (x + 0.044715 * (x * x * x))))


def _ffn_ln_kernel(with_mix, x_ref, *refs):
    x = x_ref[...]
    if with_mix:
        yab_ref, yc_ref, wab_ref, wc_ref, g2_ref, b2_ref, *refs = refs
        mix = _dot(yab_ref[...], wab_ref[...]) + _dot(yc_ref[...], wc_ref[...])
        x = _layer_norm(ALPHA * x + mix, g2_ref[...], b2_ref[...])
    wg_ref, wu_ref, wd_ref, g_ref, b_ref, o_ref, acc_ref = refs
    xb = x.astype(BF16)
    for c in range(D_FF // FF_CHUNK):
        cols = slice(c * FF_CHUNK, (c + 1) * FF_CHUNK)
        gate = _dot(xb, wg_ref[:, cols])
        up = _dot(xb, wu_ref[:, cols])
        h = (gate * jax.nn.sigmoid(gate) * up).astype(BF16)
        down = _dot(h, wd_ref[cols, :])
        if c == 0:
            acc_ref[...] = down
        else:
            acc_ref[...] += down
    y = ALPHA * x + 0.5 * acc_ref[...]
    o_ref[...] = _layer_norm(y, g_ref[...], b_ref[...])


def _ffn_ln(x2d, wg, wu, wd, g, b, mix=None):
    rows = x2d.shape[0]
    resident = functools.partial(pl.BlockSpec, pipeline_mode=pl.Buffered(1))
    row = lambda width: pl.BlockSpec((FFN_ROW_TILE, width), lambda i: (i, 0))
    vec = pl.BlockSpec((1, D_MODEL), lambda i: (0, 0))
    operands, specs = [x2d], [row(D_MODEL)]
    if mix is not None:
        yab, yc, w_out, g2, b2 = mix
        d_ab = yab.shape[1]
        w = w_out.astype(BF16)
        operands += [yab, yc, w[:d_ab], w[d_ab:], g2.reshape(1, D_MODEL), b2.reshape(1, D_MODEL)]
        specs += [row(d_ab), row(yc.shape[1]), resident((d_ab, D_MODEL), lambda i: (0, 0)),
                  resident((yc.shape[1], D_MODEL), lambda i: (0, 0)), vec, vec]
    operands += [wg.astype(BF16), wu.astype(BF16), wd.astype(BF16),
                 g.reshape(1, D_MODEL), b.reshape(1, D_MODEL)]
    specs += [resident((D_MODEL, D_FF), lambda i: (0, 0)), resident((D_MODEL, D_FF), lambda i: (0, 0)),
              resident((D_FF, D_MODEL), lambda i: (0, 0)), vec, vec]
    return pl.pallas_call(
        functools.partial(_ffn_ln_kernel, mix is not None),
        grid=(rows // FFN_ROW_TILE,),
        in_specs=specs,
        out_specs=row(D_MODEL),
        out_shape=jax.ShapeDtypeStruct((rows, D_MODEL), F32),
        scratch_shapes=[pltpu.VMEM((FFN_ROW_TILE, D_MODEL), F32)],
        compiler_params=pltpu.CompilerParams(
            dimension_semantics=("parallel",), vmem_limit_bytes=VMEM_LIMIT_FFN),
        name="mix_ffn_ln" if mix is not None else "ffn_ln",
    )(*operands)


def _proj_kernel(tiles_per_seq, x_ref, xh_ref, wconv_ref, wuv_ref, wq_ref, wkv_ref, wgl_ref,
                 convw_ref, glg_ref, glb_ref, gw_ref, gb_ref,
                 yab_ref, qt_ref, kc_ref, vc_ref, kk_ref, vvt_ref, gatet_ref, stage_ref):
    i = pl.program_id(0)
    tm = x_ref.shape[0]
    xb = x_ref[...].astype(BF16)

    hbc = _dot(xb, wconv_ref[...])
    hbc_halo = _dot(xh_ref[...].astype(BF16), wconv_ref[...])
    uv = _dot(xb, wuv_ref[...])

    z = hbc[:, 2 * D_CONV:] * hbc[:, :D_CONV]
    z_halo = hbc_halo[:, 2 * D_CONV:] * hbc_halo[:, :D_CONV]
    z_halo = jnp.where(i % tiles_per_seq == 0, 0.0, z_halo)
    row = lax.broadcasted_iota(jnp.int32, (tm, D_CONV), 0)
    z1 = jnp.where(row == 0, z_halo[7:8], pltpu.roll(z, 1, 0))
    z2 = jnp.where(row == 0, z_halo[6:7], jnp.where(row == 1, z_halo[7:8], pltpu.roll(z, 2, 0)))
    cw = convw_ref[...]
    y_a = hbc[:, D_CONV:2 * D_CONV] * (cw[0:1] * z2 + cw[1:2] * z1 + cw[2:3] * z)

    q = _dot(xb, wq_ref[...])

    zz = _gelu_tanh(uv)
    u = zz[:, :D_GMLP]
    v = _layer_norm(zz[:, D_GMLP:], glg_ref[...], glb_ref[...]).astype(BF16)
    pr = lax.broadcasted_iota(jnp.int32, (GMLP_CHUNK, GMLP_CHUNK), 0)
    pc = lax.broadcasted_iota(jnp.int32, (GMLP_CHUNK, GMLP_CHUNK), 1)
    w_tril = [jnp.where(pr >= pc, gw_ref[h], 0.0).astype(BF16) for h in range(GMLP_HEADS)]
    sv_chunks = []
    for c in range(tm // GMLP_CHUNK):
        vch = v[c * GMLP_CHUNK:(c + 1) * GMLP_CHUNK]
        parts = [_dot(w_tril[h], vch[:, h * GMLP_HEAD_DIM:(h + 1) * GMLP_HEAD_DIM])
                 for h in range(GMLP_HEADS)]
        sv_chunks.append(jnp.concatenate(parts, axis=1) + gb_ref[...])
    y_b = u * jnp.concatenate(sv_chunks, axis=0)

    kv = _dot(xb, wkv_ref[...])
    gate_logits = _dot(xb, wgl_ref[...])
    yab_ref[...] = jnp.concatenate([y_a, y_b], axis=1).astype(BF16)
    qt_ref[...] = (q * Q_SCALE).T.astype(BF16)
    stage_ref[0] = kv[:, :D_KV]
    stage_ref[1] = kv[:, D_KV:2 * D_KV]
    for which, out_ref in enumerate((kc_ref, vc_ref)):
        for off in range(CMP_STRIDE):
            out_ref[:, off * D_KV:(off + 1) * D_KV] = stage_ref[
                which, pl.ds(off, tm // CMP_STRIDE, stride=CMP_STRIDE), :].astype(BF16)
    kk_ref[...] = jnp.concatenate([kv[:, 2 * D_KV:3 * D_KV], kv[:, 4 * D_KV:5 * D_KV]], axis=1).astype(BF16)
    vvt_ref[...] = jnp.concatenate([kv[:, 3 * D_KV:4 * D_KV], kv[:, 5 * D_KV:]], axis=1).T.astype(BF16)
    gatet_ref[...] = jax.nn.sigmoid(gate_logits).T


def _proj(x2d, seq, w_in, conv_w, gln_g, gln_b, gmlp_w, gmlp_b):
    rows = x2d.shape[0]
    tm = ROW_TILE
    w = w_in.astype(BF16)
    o = 0
    wconv = w[:, o:o + 3 * D_CONV]; o += 3 * D_CONV
    wuv = w[:, o:o + 2 * D_GMLP]; o += 2 * D_GMLP
    wq = w[:, o:o + D_NSA]; o += D_NSA
    wkv = w[:, o:o + 6 * D_KV]; o += 6 * D_KV
    n_gate = 3 * NSA_HEADS
    wgl = jnp.pad(w[:, o:o + n_gate], ((0, 0), (0, 128 - n_gate)))
    gb_exp = jnp.repeat(gmlp_b.T, GMLP_HEAD_DIM, axis=1)
    const = lambda shape: pl.BlockSpec(shape, lambda i: (0,) * len(shape))
    row_spec = lambda width: pl.BlockSpec((tm, width), lambda i: (i, 0))
    col_spec = lambda height: pl.BlockSpec((height, tm), lambda i: (0, i))
    strided_spec = pl.BlockSpec((tm // CMP_STRIDE, CMP_STRIDE * D_KV), lambda i: (i, 0))
    return pl.pallas_call(
        functools.partial(_proj_kernel, seq // tm),
        grid=(rows // tm,),
        in_specs=[
            row_spec(D_MODEL),
            pl.BlockSpec((8, D_MODEL), lambda i: (jnp.maximum(i * (tm // 8) - 1, 0), 0)),
            const(wconv.shape), const(wuv.shape), const(wq.shape), const(wkv.shape), const(wgl.shape),
            const((CONV_WIDTH, D_CONV)), const((1, D_GMLP)), const((1, D_GMLP)),
            const((GMLP_HEADS, GMLP_CHUNK, GMLP_CHUNK)), const((GMLP_CHUNK, D_GMLP)),
        ],
        out_specs=[row_spec(D_CONV + D_GMLP), col_spec(D_NSA), strided_spec, strided_spec,
                   row_spec(2 * D_KV), col_spec(2 * D_KV), col_spec(128)],
        out_shape=[
            jax.ShapeDtypeStruct((rows, D_CONV + D_GMLP), BF16),
            jax.ShapeDtypeStruct((D_NSA, rows), BF16),
            jax.ShapeDtypeStruct((rows // CMP_STRIDE, CMP_STRIDE * D_KV), BF16),
            jax.ShapeDtypeStruct((rows // CMP_STRIDE, CMP_STRIDE * D_KV), BF16),
            jax.ShapeDtypeStruct((rows, 2 * D_KV), BF16),
            jax.ShapeDtypeStruct((2 * D_KV, rows), BF16),
            jax.ShapeDtypeStruct((128, rows), F32),
        ],
        scratch_shapes=[pltpu.VMEM((2, tm, D_KV), F32)],
        compiler_params=pltpu.CompilerParams(
            dimension_semantics=("parallel",), vmem_limit_bytes=VMEM_LIMIT),
        name="in_proj",
    )(x2d, x2d, wconv, wuv, wq, wkv, wgl, conv_w, gln_g.reshape(1, D_GMLP), gln_b.reshape(1, D_GMLP),
      gmlp_w, gb_exp)


def _compress_kernel(k_ref, v_ref, w1k_ref, w1v_ref, pek_ref, pev_ref, b1k_ref, b1v_ref,
                     w2k_ref, w2v_ref, kc_ref, vct_ref):
    def one(x2d, w1, pe2, b1, w2):
        n = x2d.shape[0]
        r = _dot(x2d, w1)
        cst = _dot(pe2, w1)
        outs = []
        for g in range(NSA_KV_HEADS):
            lo = slice(g * CMP_HIDDEN, (g + 1) * CMP_HIDDEN)
            hi = slice((NSA_KV_HEADS + g) * CMP_HIDDEN, (NSA_KV_HEADS + g + 1) * CMP_HIDDEN)
            first = r[:, lo] + cst[0:1, lo]
            second = r[:, hi] + cst[1:2, hi]
            h = _gelu_tanh(first + pltpu.roll(second, n - 1, 0) + b1)
            outs.append(_dot(h.astype(BF16), w2))
        return jnp.concatenate(outs, axis=1)

    kc_ref[0] = one(k_ref[0], w1k_ref[...], pek_ref[...], b1k_ref[...], w2k_ref[...]).astype(BF16)
    vct_ref[0] = one(v_ref[0], w1v_ref[...], pev_ref[...], b1v_ref[...], w2v_ref[...]).T.astype(BF16)


def _expand_w1(w1):
    half = CMP_LEN // 2
    eye = jnp.eye(NSA_KV_HEADS, dtype=w1.dtype)
    w = w1.reshape(2, half, HEAD_DIM, CMP_HIDDEN)
    e = jnp.einsum("aldf,hg->lhdagf", w, eye)
    return e.reshape(half * NSA_KV_HEADS * HEAD_DIM, 2 * NSA_KV_HEADS * CMP_HIDDEN).astype(BF16)


def _expand_pe(pe):
    half = CMP_LEN // 2
    p = pe.reshape(2, half, 1, HEAD_DIM)
    p = jnp.broadcast_to(p, (2, half, NSA_KV_HEADS, HEAD_DIM)).reshape(2, -1)
    return jnp.pad(p, ((0, 6), (0, 0))).astype(BF16)


def _compress(kc2d, vc2d, batch, seq, ck_pe, ck_w1, ck_b1, ck_w2, cv_pe, cv_w1, cv_b1, cv_w2):
    n = seq // CMP_STRIDE
    width = CMP_STRIDE * D_KV
    k3 = kc2d.reshape(batch, n, width)
    v3 = vc2d.reshape(batch, n, width)
    const = lambda shape: pl.BlockSpec(shape, lambda b: (0,) * len(shape))
    blk = pl.BlockSpec((1, n, width), lambda b: (b, 0, 0))
    out = pl.BlockSpec((1, n, D_KV), lambda b: (b, 0, 0))
    out_t = pl.BlockSpec((1, D_KV, n), lambda b: (b, 0, 0))
    wshape = (width, 2 * NSA_KV_HEADS * CMP_HIDDEN)
    return pl.pallas_call(
        _compress_kernel,
        grid=(batch,),
        in_specs=[blk, blk, const(wshape), const(wshape), const((8, width)), const((8, width)),
                  const((1, CMP_HIDDEN)), const((1, CMP_HIDDEN)),
                  const((CMP_HIDDEN, HEAD_DIM)), const((CMP_HIDDEN, HEAD_DIM))],
        out_specs=[out, out_t],
        out_shape=[jax.ShapeDtypeStruct((batch, n, D_KV), BF16),
                   jax.ShapeDtypeStruct((batch, D_KV, n), BF16)],
        compiler_params=pltpu.CompilerParams(
            dimension_semantics=("parallel",), vmem_limit_bytes=VMEM_LIMIT),
        name="compress_kv",
    )(k3, v3, _expand_w1(ck_w1), _expand_w1(cv_w1), _expand_pe(ck_pe), _expand_pe(cv_pe),
      ck_b1.reshape(1, CMP_HIDDEN), cv_b1.reshape(1, CMP_HIDDEN),
      ck_w2.astype(BF16), cv_w2.astype(BF16))


def _pad_group(x, g):
    z = jnp.zeros_like(x)
    return jnp.concatenate([x if gg == g else z for gg in range(NSA_KV_HEADS)], axis=0)


def _cmp_attn_kernel(qt_ref, kc_ref, vct_ref, ovt_ref, ocmpt_ref, selt_ref, score_ref, rank_ref):
    tq = qt_ref.shape[1]
    n_cmp = kc_ref.shape[1]
    n_blk = ovt_ref.shape[0]
    t0 = pl.program_id(1) * tq
    tq_pos = t0 + lax.broadcasted_iota(jnp.int32, (1, tq), 1)
    inv_gate = jnp.where(tq_pos >= CMP_LEN - 1, 1.0, 0.0)

    jj = lax.broadcasted_iota(jnp.int32, (n_blk, tq), 0)
    cur = jnp.right_shift(t0 + lax.broadcasted_iota(jnp.int32, (n_blk, tq), 1), SEL_SHIFT)
    valid = jj <= cur
    forced = (jj == 0) | (jj == cur) | (jj == cur - 1)
    sub8 = lax.broadcasted_iota(jnp.int32, (8, tq), 0)

    def attend(n_vis):
        kc = kc_ref[0, :n_vis, :]
        visible = (lax.broadcasted_iota(jnp.int32, (n_vis, tq), 0) * CMP_STRIDE
                   + (CMP_LEN - 1)) <= tq_pos

        def scores(h):
            return _dot(kc, _pad_group(qt_ref[h * HEAD_DIM:(h + 1) * HEAD_DIM, :], h // NSA_GROUP))

        s_next = scores(0)
        for h in range(NSA_HEADS):
            g, r = divmod(h, NSA_GROUP)
            s = jnp.where(visible, s_next, NEG)
            if h + 1 < NSA_HEADS:
                s_next = scores(h + 1)
            e = jnp.exp2(s - jnp.max(s, axis=0, keepdims=True))
            p = e * (inv_gate / jnp.sum(e, axis=0, keepdims=True))
            ocmpt_ref[h * HEAD_DIM:(h + 1) * HEAD_DIM, :] = _dot(
                vct_ref[0, g * HEAD_DIM:(g + 1) * HEAD_DIM, :n_vis], p.astype(BF16))
            psum = p if r == 0 else psum + p
            if r == NSA_GROUP - 1:
                hi = psum.astype(BF16)
                lo = (psum - hi.astype(F32)).astype(BF16)
                imp_t = _dot(ovt_ref[:, :n_vis], hi) + _dot(ovt_ref[:, :n_vis], lo)
                score_ref[g] = jnp.where(forced, -NEG, jnp.where(valid, imp_t, NEG))

    half = n_cmp // 2
    if half % 128 == 0:
        last_visible = (t0 + tq - CMP_LEN) // CMP_STRIDE
        lax.cond(last_visible < half, lambda: attend(half), lambda: attend(n_cmp))
    else:
        attend(n_cmp)

    n_grp = n_blk // 8
    rank_ref[...] = jnp.zeros(rank_ref.shape, F32)
    last_grp = jnp.right_shift(t0 + tq - 1, SEL_SHIFT + 3)

    def count(g, gi, gj):
        rows, blk = score_ref[g, 8 * gi:8 * gi + 8], score_ref[g, 8 * gj:8 * gj + 8]
        total = rank_ref[g, 8 * gj:8 * gj + 8]
        for k in range(8):
            row = jnp.broadcast_to(rows[k:k + 1, :], (8, tq))
            if gi < gj:
                before = jnp.where(row >= blk, 1.0, 0.0)
            elif gi > gj:
                before = jnp.where(row > blk, 1.0, 0.0)
            else:
                before = jnp.where(sub8 > k, jnp.where(row >= blk, 1.0, 0.0),
                                   jnp.where(row > blk, 1.0, 0.0))
            total = total + before
        rank_ref[g, 8 * gj:8 * gj + 8] = total

    for hi_grp in range(n_grp):
        @pl.when(last_grp >= hi_grp)
        def _(hi_grp=hi_grp):
            for g in range(NSA_KV_HEADS):
                for lo_grp in range(hi_grp):
                    count(g, hi_grp, lo_grp)
                    count(g, lo_grp, hi_grp)
                count(g, hi_grp, hi_grp)

    for g in range(NSA_KV_HEADS):
        selt_ref[g * n_blk:(g + 1) * n_blk, :] = jnp.where(
            rank_ref[g] < float(min(N_SELECT, n_blk)), 1.0, 0.0).astype(BF16)


def _overlap_t(n_cmp_pad, n_blk):
    st = np.arange(n_cmp_pad) * CMP_STRIDE
    bs = np.arange(n_blk) * SEL_BLOCK
    ov = np.minimum(st[:, None] + CMP_LEN, bs[None, :] + SEL_BLOCK) - np.maximum(st[:, None], bs[None, :])
    return jnp.asarray((np.clip(ov, 0, None) / CMP_LEN).T, dtype=BF16)


def _cmp_attn(qt, kc, vct, batch, seq):
    tq = CMP_Q_TILE
    n_cmp = kc.shape[1]
    n_blk = seq // SEL_BLOCK
    per = seq // tq
    cols = lambda height: pl.BlockSpec((height, tq), lambda b, j: (0, b * per + j))
    return pl.pallas_call(
        _cmp_attn_kernel,
        grid=(batch, per),
        in_specs=[
            cols(D_NSA),
            pl.BlockSpec((1, n_cmp, D_KV), lambda b, j: (b, 0, 0)),
            pl.BlockSpec((1, D_KV, n_cmp), lambda b, j: (b, 0, 0)),
            pl.BlockSpec((n_blk, n_cmp), lambda b, j: (0, 0)),
        ],
        out_specs=[cols(D_NSA), cols(NSA_KV_HEADS * n_blk)],
        out_shape=[jax.ShapeDtypeStruct((D_NSA, batch * seq), F32),
                   jax.ShapeDtypeStruct((NSA_KV_HEADS * n_blk, batch * seq), BF16)],
        scratch_shapes=[pltpu.VMEM((NSA_KV_HEADS, n_blk, tq), F32)] * 2,
        compiler_params=pltpu.CompilerParams(
            dimension_semantics=("parallel", "parallel"), vmem_limit_bytes=VMEM_LIMIT),
        name="cmp_attn_select",
    )(qt, kc, vct, _overlap_t(n_cmp, n_blk))


def _sel_win_kernel(qt_ref, kk_ref, vvt_ref, selt_ref, ocmpt_ref, gatet_ref, out_ref,
                    bias_ref, acc_ref, qaug_ref, sa_ref, sb_ref):
    qb = qt_ref.shape[1]
    n_blk = selt_ref.shape[0] // NSA_KV_HEADS
    kc = SEL_KEY_CHUNK
    lanes = NSA_GROUP * qb
    s0 = pl.program_id(1) * qb
    n_full = s0 // kc
    win_keys = WINDOW + qb
    win_start = pl.multiple_of(jnp.maximum(s0 - WINDOW, 0), qb)

    blk_per_chunk = kc // SEL_BLOCK
    bias = jnp.where(selt_ref[...].astype(F32) > 0.5, 0.0, NEG)
    if blk_per_chunk < 8:
        bias_ref[...] = jnp.zeros(bias_ref.shape, F32)
    for c in range(bias_ref.shape[0]):
        bias_ref[c, :blk_per_chunk, :] = bias[c * blk_per_chunk:(c + 1) * blk_per_chunk]
    tq =s0 + lax.broadcasted_iota(jnp.int32, (1, qb), 1)
    dist = tq - (win_start + lax.broadcasted_iota(jnp.int32, (win_keys, qb), 0))
    wbias = jnp.where((dist >= 0) & (dist < WINDOW), 0.0, NEG)
    wbias = jnp.concatenate([wbias] * NSA_GROUP, axis=1)
    gt = gatet_ref[...]

    groups = range(NSA_KV_HEADS)
    qg = [_pad_group(jnp.concatenate(
        [qt_ref[h * HEAD_DIM:(h + 1) * HEAD_DIM, :]
         for h in range(g * NSA_GROUP, (g + 1) * NSA_GROUP)], axis=1), g) for g in groups]
    onehot = jnp.where(
        jnp.right_shift(lax.broadcasted_iota(jnp.int32, (kc, D_KV), 0), SEL_SHIFT)
        == lax.broadcasted_iota(jnp.int32, (kc, D_KV), 1), 1.0, 0.0).astype(BF16)
    for g in groups:
        qaug_ref[g, :D_KV, :] = qg[g]
        qaug_ref[g, D_KV:, :] = jnp.zeros((D_KV, lanes), BF16)

    def scores(g, k0):
        rows = jnp.concatenate([bias_ref[g * (n_blk // blk_per_chunk) + k0 // kc],
                                jnp.zeros((8, qb), F32)], axis=0).astype(BF16)
        qaug_ref[g, D_KV:D_KV + 16, :] = jnp.concatenate([rows] * NSA_GROUP, axis=1)
        if not isinstance(k0, int):
            k0 = pl.multiple_of(k0, kc)
        keys = jnp.concatenate([kk_ref[pl.ds(k0, kc), :D_KV], onehot], axis=1)
        return _dot(keys, qaug_ref[g])

    def update(g, k0, s, m, l):
        m_new = jnp.maximum(m, jnp.max(s, axis=0, keepdims=True))
        alpha = jnp.exp2(m - m_new)
        p = jnp.exp2(s - m_new)
        v = vvt_ref[g * HEAD_DIM:(g + 1) * HEAD_DIM, pl.ds(k0, kc)]
        acc_ref[g] = alpha * acc_ref[g] + _dot(v, p.astype(BF16))
        return m_new, alpha * l + jnp.sum(p, axis=0, keepdims=True)

    def score_into(ref, c):
        for g in groups:
            ref[g] = scores(g, c * kc)

    def softmax_from(ref, c, carry):
        k0 = pl.multiple_of(c * kc, kc)
        return tuple(update(g, k0, ref[g], *carry[g]) for g in groups)

    acc_ref[...] = jnp.zeros(acc_ref.shape, F32)
    init = ((jnp.full((1, lanes), NEG, F32), jnp.zeros((1, lanes), F32)),) * NSA_KV_HEADS
    odd = lax.rem(n_full, 2)

    def peel_odd():
        score_into(sb_ref, 0)
        score_into(sa_ref, 1)
        return softmax_from(sb_ref, 0, init)

    def peel_even():
        score_into(sa_ref, 0)
        return init

    carry = lax.cond(odd == 1, peel_odd, peel_even)

    def body(i, carry):
        c = 2 * i + odd
        score_into(sb_ref, c + 1)
        carry = softmax_from(sa_ref, c, carry)
        score_into(sa_ref, c + 2)
        return softmax_from(sb_ref, c + 1, carry)

    carry = lax.fori_loop(0, n_full // 2, body, carry)
    k_last = pl.multiple_of(n_full * kc, kc)
    seen = k_last + lax.broadcasted_iota(jnp.int32, (kc, qb), 0) <= tq

    def last_scores(g):
        s = sa_ref[g]
        return jnp.concatenate([jnp.where(seen, s[:, r * qb:(r + 1) * qb], NEG)
                                for r in range(NSA_GROUP)], axis=1)

    def window_scores(g):
        return _dot(kk_ref[pl.ds(win_start, win_keys), D_KV:], qg[g]) + wbias

    def window_out(g, sw):
        pw = jnp.exp2(sw - jnp.max(sw, axis=0, keepdims=True))
        vw = vvt_ref[D_KV + g * HEAD_DIM:D_KV + (g + 1) * HEAD_DIM, pl.ds(win_start, win_keys)]
        return _dot(vw, pw.astype(BF16)) * (1.0 / jnp.sum(pw, axis=0, keepdims=True))

    o_sel, o_win, s_win = [], [], [None] * NSA_KV_HEADS
    for g in groups:
        s_win[g] = window_scores(g)
        _, l = update(g, k_last, last_scores(g), *carry[g])
        o_sel.append(acc_ref[g] * (1.0 / l))
    for g in groups:
        o_win.append(window_out(g, s_win[g]))

    pieces = []
    for g in groups:
        o_s, o_w = o_sel[g], o_win[g]
        for r in range(NSA_GROUP):
            h = g * NSA_GROUP + r
            ln = slice(r * qb, (r + 1) * qb)
            pieces.append(gt[3 * h:3 * h + 1] * ocmpt_ref[h * HEAD_DIM:(h + 1) * HEAD_DIM, :]
                          + gt[3 * h + 1:3 * h + 2] * o_s[:, ln]
                          + gt[3 * h + 2:3 * h + 3] * o_w[:, ln])
    out_ref[...] = jnp.concatenate(pieces, axis=0).T.astype(BF16)


def _sel_win(qt, kk, vvt, selt, ocmpt, gatet, batch, seq):
    qb = Q_BLOCK
    per = seq // qb
    cols = lambda height: pl.BlockSpec((height, qb), lambda b, i: (0, b * per + i))
    return pl.pallas_call(
        _sel_win_kernel,
        grid=(batch, per),
        in_specs=[cols(D_NSA),
                  pl.BlockSpec((seq, 2 * D_KV), lambda b, i: (b, 0)),
                  pl.BlockSpec((2 * D_KV, seq), lambda b, i: (0, b)),
                  cols(selt.shape[0]), cols(D_NSA), cols(128)],
        out_specs=pl.BlockSpec((qb, D_NSA), lambda b, i: (b * per + i, 0)),
        out_shape=jax.ShapeDtypeStruct((batch * seq, D_NSA), BF16),
        scratch_shapes=[pltpu.VMEM((selt.shape[0] * SEL_BLOCK // SEL_KEY_CHUNK, 8, qb), F32),
                        pltpu.VMEM((NSA_KV_HEADS, HEAD_DIM, NSA_GROUP * qb), F32),
                        pltpu.VMEM((NSA_KV_HEADS, 2 * D_KV, NSA_GROUP * qb), BF16),
                        pltpu.VMEM((NSA_KV_HEADS, SEL_KEY_CHUNK, NSA_GROUP * qb), F32),
                        pltpu.VMEM((NSA_KV_HEADS, SEL_KEY_CHUNK, NSA_GROUP * qb), F32)],
        compiler_params=pltpu.CompilerParams(
            dimension_semantics=("parallel", "arbitrary"), vmem_limit_bytes=VMEM_LIMIT),
        name="sel_win_attn",
    )(qt, kk, vvt, selt, ocmpt, gatet)


def kernel(x, ffn1_gate, ffn1_up, ffn1_down, ln1_g, ln1_b, w_in, conv_w, gmlp_ln_g, gmlp_ln_b, gmlp_w, gmlp_b, ck_pe, ck_w1, ck_b1, ck_w2, cv_pe, cv_w1, cv_b1, cv_w2, w_out, ln2_g, ln2_b, ffn2_gate, ffn2_up, ffn2_down, ln3_g, ln3_b):
    batch, seq, d = x.shape
    assert d == D_MODEL and seq % max(ROW_TILE, FFN_ROW_TILE, CMP_Q_TILE, SEL_KEY_CHUNK) == 0
    h = x.reshape(batch * seq, d)
    for l in range(ffn1_gate.shape[0]):
        h = _ffn_ln(h, ffn1_gate[l], ffn1_up[l], ffn1_down[l], ln1_g[l], ln1_b[l])
        yab, qt, kcr, vcr, kk, vvt, gatet = _proj(h, seq, w_in[l], conv_w[l], gmlp_ln_g[l],
                                                  gmlp_ln_b[l], gmlp_w[l], gmlp_b[l])
        kc, vct = _compress(kcr, vcr, batch, seq, ck_pe[l], ck_w1[l], ck_b1[l], ck_w2[l],
                            cv_pe[l], cv_w1[l], cv_b1[l], cv_w2[l])
        ocmpt, selt = _cmp_attn(qt, kc, vct, batch, seq)
        yc = _sel_win(qt, kk, vvt, selt, ocmpt, gatet, batch, seq)
        h = _ffn_ln(h, ffn2_gate[l], ffn2_up[l], ffn2_down[l], ln3_g[l], ln3_b[l],
                    mix=(yab, yc, w_out[l], ln2_g[l], ln2_b[l]))
    return h.reshape(batch, seq, d)
```

```python
import functools

import numpy as np
import jax
import jax.numpy as jnp
from jax import lax
from jax.experimental import pallas as pl
from jax.experimental.pallas import tpu as pltpu

F32 = jnp.float32
BF16 = jnp.bfloat16

D_MODEL = 1024
D_FF = 2816
D_CONV = 256
CONV_WIDTH = 3
D_GMLP = 256
GMLP_HEADS = 4
GMLP_HEAD_DIM = D_GMLP // GMLP_HEADS
GMLP_CHUNK = 128
HEAD_DIM = 64
NSA_HEADS = 8
NSA_KV_HEADS = 2
NSA_GROUP = NSA_HEADS // NSA_KV_HEADS
D_NSA = NSA_HEADS * HEAD_DIM
D_KV = NSA_KV_HEADS * HEAD_DIM
CMP_LEN = 32
CMP_STRIDE = 16
CMP_HIDDEN = 128
SEL_BLOCK = 64
SEL_SHIFT = 6
N_SELECT = 16
WINDOW = 512
DEPTH = 2
ALPHA = (2 * DEPTH) ** 0.25
LN_EPS = 1e-5
NEG = -1e30
LOG2E = 1.4426950408889634
Q_SCALE = HEAD_DIM ** -0.5 * LOG2E

FF_CHUNK = 256
ROW_TILE = 512
FFN_ROW_TILE = 1024
FFN_SUB_TILE = 512
CMP_Q_TILE = 512
Q_BLOCK = 256
SEL_KEY_CHUNK = 512
VMEM_LIMIT_FFN = 56 * 1024 * 1024
VMEM_LIMIT = 48 * 1024 * 1024


def _dot(a, b):
    return jnp.dot(a, b, preferred_element_type=F32)


def _dot_nt(a, b):
    return lax.dot_general(a, b, (((1,), (1,)), ((), ())), preferred_element_type=F32)


def _layer_norm(y, g, b):
    mu = jnp.mean(y, axis=-1, keepdims=True)
    d = y - mu
    var = jnp.mean(d * d, axis=-1, keepdims=True)
    return d * lax.rsqrt(var + LN_EPS) * g + b


def _gelu_tanh(x):
    return 0.5 * x * (1.0 + jnp.tanh(0.7978845608028654 * (x + 0.044715 * (x * x * x))))


def _ffn_ln_kernel(with_mix, x_ref, *refs):
    if with_mix:
        yab_ref, yc_ref, wab_ref, wc_ref, g2_ref, b2_ref, *refs = refs
    wg_ref, wu_ref, wd_ref, g_ref, b_ref, o_ref, acc_ref = refs
    for t in range(x_ref.shape[0] // FFN_SUB_TILE):
        rows = slice(t * FFN_SUB_TILE, (t + 1) * FFN_SUB_TILE)
        x = x_ref[rows, :]
        if with_mix:
            mix = _dot(yab_ref[rows, :], wab_ref[...]) + _dot(yc_ref[rows, :], wc_ref[...])
            x = _layer_norm(ALPHA * x + mix, g2_ref[...], b2_ref[...])
        xb = x.astype(BF16)
        for c in range(D_FF // FF_CHUNK):
            cols = slice(c * FF_CHUNK, (c + 1) * FF_CHUNK)
            gate = _dot(xb, wg_ref[:, cols])
            up = _dot(xb, wu_ref[:, cols])
            h = (gate * jax.nn.sigmoid(gate) * up).astype(BF16)
            down = _dot(h, wd_ref[cols, :])
            if c == 0:
                acc_ref[rows, :] = down
            else:
                acc_ref[rows, :] += down
        y = ALPHA * x + 0.5 * acc_ref[rows, :]
        o_ref[rows, :] = _layer_norm(y, g_ref[...], b_ref[...])


def _ffn_ln(x2d, wg, wu, wd, g, b, mix=None):
    rows = x2d.shape[0]
    resident = functools.partial(pl.BlockSpec, pipeline_mode=pl.Buffered(1))
    row = lambda width: pl.BlockSpec((FFN_ROW_TILE, width), lambda i: (i, 0))
    vec = pl.BlockSpec((1, D_MODEL), lambda i: (0, 0))
    operands, specs = [x2d], [row(D_MODEL)]
    if mix is not None:
        yab, yc, w_out, g2, b2 = mix
        d_ab = yab.shape[1]
        w = w_out.astype(BF16)
        operands += [yab, yc, w[:d_ab], w[d_ab:], g2.reshape(1, D_MODEL), b2.reshape(1, D_MODEL)]
        specs += [row(d_ab), row(yc.shape[1]), resident((d_ab, D_MODEL), lambda i: (0, 0)),
                  resident((yc.shape[1], D_MODEL), lambda i: (0, 0)), vec, vec]
    operands += [wg.astype(BF16), wu.astype(BF16), wd.astype(BF16),
                 g.reshape(1, D_MODEL), b.reshape(1, D_MODEL)]
    specs += [resident((D_MODEL, D_FF), lambda i: (0, 0)), resident((D_MODEL, D_FF), lambda i: (0, 0)),
              resident((D_FF, D_MODEL), lambda i: (0, 0)), vec, vec]
    return pl.pallas_call(
        functools.partial(_ffn_ln_kernel, mix is not None),
        grid=(rows // FFN_ROW_TILE,),
        in_specs=specs,
        out_specs=row(D_MODEL),
        out_shape=jax.ShapeDtypeStruct((rows, D_MODEL), F32),
        scratch_shapes=[pltpu.VMEM((FFN_ROW_TILE, D_MODEL), F32)],
        compiler_params=pltpu.CompilerParams(
            dimension_semantics=("parallel",), vmem_limit_bytes=VMEM_LIMIT_FFN),
        name="mix_ffn_ln" if mix is not None else "ffn_ln",
    )(*operands)


def _proj_kernel(tiles_per_seq, x_ref, xh_ref, wconv_ref, wuv_ref, wq_ref, wkv_ref, wgl_ref,
                 convw_ref, glg_ref, glb_ref, gw_ref, gb_ref,
                 yab_ref, qt_ref, kc_ref, vc_ref, kk_ref, vvt_ref, gatet_ref, stage_ref):
    i = pl.program_id(0)
    tm = x_ref.shape[0]
    xb = x_ref[...].astype(BF16)

    hbc = _dot(xb, wconv_ref[...])
    hbc_halo = _dot(xh_ref[...].astype(BF16), wconv_ref[...])
    uv = _dot(xb, wuv_ref[...])

    z = hbc[:, 2 * D_CONV:] * hbc[:, :D_CONV]
    z_halo = hbc_halo[:, 2 * D_CONV:] * hbc_halo[:, :D_CONV]
    z_halo = jnp.where(i % tiles_per_seq == 0, 0.0, z_halo)
    row = lax.broadcasted_iota(jnp.int32, (tm, D_CONV), 0)
    z1 = jnp.where(row == 0, z_halo[7:8], pltpu.roll(z, 1, 0))
    z2 = jnp.where(row == 0, z_halo[6:7], jnp.where(row == 1, z_halo[7:8], pltpu.roll(z, 2, 0)))
    cw = convw_ref[...]
    y_a = hbc[:, D_CONV:2 * D_CONV] * (cw[0:1] * z2 + cw[1:2] * z1 + cw[2:3] * z)

    q = _dot(xb, wq_ref[...])

    zz = _gelu_tanh(uv)
    u = zz[:, :D_GMLP]
    v = _layer_norm(zz[:, D_GMLP:], glg_ref[...], glb_ref[...]).astype(BF16)
    pr = lax.broadcasted_iota(jnp.int32, (GMLP_CHUNK, GMLP_CHUNK), 0)
    pc = lax.broadcasted_iota(jnp.int32, (GMLP_CHUNK, GMLP_CHUNK), 1)
    w_tril = [jnp.where(pr >= pc, gw_ref[h], 0.0).astype(BF16) for h in range(GMLP_HEADS)]
    sv_chunks = []
    for c in range(tm // GMLP_CHUNK):
        vch = v[c * GMLP_CHUNK:(c + 1) * GMLP_CHUNK]
        parts = [_dot(w_tril[h], vch[:, h * GMLP_HEAD_DIM:(h + 1) * GMLP_HEAD_DIM])
                 for h in range(GMLP_HEADS)]
        sv_chunks.append(jnp.concatenate(parts, axis=1) + gb_ref[...])
    y_b = u * jnp.concatenate(sv_chunks, axis=0)

    kv = _dot(xb, wkv_ref[...])
    gate_logits = _dot(xb, wgl_ref[...])
    yab_ref[...] = jnp.concatenate([y_a, y_b], axis=1).astype(BF16)
    qt_ref[...] = (q * Q_SCALE).T.astype(BF16)
    stage_ref[0] = kv[:, :D_KV]
    stage_ref[1] = kv[:, D_KV:2 * D_KV]
    for which, out_ref in enumerate((kc_ref, vc_ref)):
        for off in range(CMP_STRIDE):
            out_ref[:, off * D_KV:(off + 1) * D_KV] = stage_ref[
                which, pl.ds(off, tm // CMP_STRIDE, stride=CMP_STRIDE), :].astype(BF16)
    kk_ref[...] = jnp.concatenate([kv[:, 2 * D_KV:3 * D_KV], kv[:, 4 * D_KV:5 * D_KV]], axis=1).astype(BF16)
    vvt_ref[...] = jnp.concatenate([kv[:, 3 * D_KV:4 * D_KV], kv[:, 5 * D_KV:]], axis=1).T.astype(BF16)
    gatet_ref[...] = jax.nn.sigmoid(gate_logits).T


def _proj(x2d, seq, w_in, conv_w, gln_g, gln_b, gmlp_w, gmlp_b):
    rows = x2d.shape[0]
    tm = ROW_TILE
    w = w_in.astype(BF16)
    o = 0
    wconv = w[:, o:o + 3 * D_CONV]; o += 3 * D_CONV
    wuv = w[:, o:o + 2 * D_GMLP]; o += 2 * D_GMLP
    wq = w[:, o:o + D_NSA]; o += D_NSA
    wkv = w[:, o:o + 6 * D_KV]; o += 6 * D_KV
    n_gate = 3 * NSA_HEADS
    wgl = jnp.pad(w[:, o:o + n_gate], ((0, 0), (0, 128 - n_gate)))
    gb_exp = jnp.repeat(gmlp_b.T, GMLP_HEAD_DIM, axis=1)
    const = lambda shape: pl.BlockSpec(shape, lambda i: (0,) * len(shape))
    row_spec = lambda width: pl.BlockSpec((tm, width), lambda i: (i, 0))
    col_spec = lambda height: pl.BlockSpec((height, tm), lambda i: (0, i))
    strided_spec = pl.BlockSpec((tm // CMP_STRIDE, CMP_STRIDE * D_KV), lambda i: (i, 0))
    return pl.pallas_call(
        functools.partial(_proj_kernel, seq // tm),
        grid=(rows // tm,),
        in_specs=[
            row_spec(D_MODEL),
            pl.BlockSpec((8, D_MODEL), lambda i: (jnp.maximum(i * (tm // 8) - 1, 0), 0)),
            const(wconv.shape), const(wuv.shape), const(wq.shape), const(wkv.shape), const(wgl.shape),
            const((CONV_WIDTH, D_CONV)), const((1, D_GMLP)), const((1, D_GMLP)),
            const((GMLP_HEADS, GMLP_CHUNK, GMLP_CHUNK)), const((GMLP_CHUNK, D_GMLP)),
        ],
        out_specs=[row_spec(D_CONV + D_GMLP), col_spec(D_NSA), strided_spec, strided_spec,
                   row_spec(2 * D_KV), col_spec(2 * D_KV), col_spec(128)],
        out_shape=[
            jax.ShapeDtypeStruct((rows, D_CONV + D_GMLP), BF16),
            jax.ShapeDtypeStruct((D_NSA, rows), BF16),
            jax.ShapeDtypeStruct((rows // CMP_STRIDE, CMP_STRIDE * D_KV), BF16),
            jax.ShapeDtypeStruct((rows // CMP_STRIDE, CMP_STRIDE * D_KV), BF16),
            jax.ShapeDtypeStruct((rows, 2 * D_KV), BF16),
            jax.ShapeDtypeStruct((2 * D_KV, rows), BF16),
            jax.ShapeDtypeStruct((128, rows), F32),
        ],
        scratch_shapes=[pltpu.VMEM((2, tm, D_KV), F32)],
        compiler_params=pltpu.CompilerParams(
            dimension_semantics=("parallel",), vmem_limit_bytes=VMEM_LIMIT),
        name="in_proj",
    )(x2d, x2d, wconv, wuv, wq, wkv, wgl, conv_w, gln_g.reshape(1, D_GMLP), gln_b.reshape(1, D_GMLP),
      gmlp_w, gb_exp)


def _compress_kernel(k_ref, v_ref, w1k_ref, w1v_ref, pek_ref, pev_ref, b1k_ref, b1v_ref,
                     w2k_ref, w2v_ref, kc_ref, vct_ref):
    def one(x2d, w1, pe2, b1, w2):
        n = x2d.shape[0]
        r = _dot(x2d, w1)
        cst = _dot(pe2, w1)
        outs = []
        for g in range(NSA_KV_HEADS):
            lo = slice(g * CMP_HIDDEN, (g + 1) * CMP_HIDDEN)
            hi = slice((NSA_KV_HEADS + g) * CMP_HIDDEN, (NSA_KV_HEADS + g + 1) * CMP_HIDDEN)
            first = r[:, lo] + cst[0:1, lo]
            second = r[:, hi] + cst[1:2, hi]
            h = _gelu_tanh(first + pltpu.roll(second, n - 1, 0) + b1)
            outs.append(_dot(h.astype(BF16), w2))
        return jnp.concatenate(outs, axis=1)

    kc_ref[0] = one(k_ref[0], w1k_ref[...], pek_ref[...], b1k_ref[...], w2k_ref[...]).astype(BF16)
    vct_ref[0] = one(v_ref[0], w1v_ref[...], pev_ref[...], b1v_ref[...], w2v_ref[...]).T.astype(BF16)


def _expand_w1(w1):
    half = CMP_LEN // 2
    eye = jnp.eye(NSA_KV_HEADS, dtype=w1.dtype)
    w = w1.reshape(2, half, HEAD_DIM, CMP_HIDDEN)
    e = jnp.einsum("aldf,hg->lhdagf", w, eye)
    return e.reshape(half * NSA_KV_HEADS * HEAD_DIM, 2 * NSA_KV_HEADS * CMP_HIDDEN).astype(BF16)


def _expand_pe(pe):
    half = CMP_LEN // 2
    p = pe.reshape(2, half, 1, HEAD_DIM)
    p = jnp.broadcast_to(p, (2, half, NSA_KV_HEADS, HEAD_DIM)).reshape(2, -1)
    return jnp.pad(p, ((0, 6), (0, 0))).astype(BF16)


def _compress(kc2d, vc2d, batch, seq, ck_pe, ck_w1, ck_b1, ck_w2, cv_pe, cv_w1, cv_b1, cv_w2):
    n = seq // CMP_STRIDE
    width = CMP_STRIDE * D_KV
    k3 = kc2d.reshape(batch, n, width)
    v3 = vc2d.reshape(batch, n, width)
    const = lambda shape: pl.BlockSpec(shape, lambda b: (0,) * len(shape))
    blk = pl.BlockSpec((1, n, width), lambda b: (b, 0, 0))
    out = pl.BlockSpec((1, n, D_KV), lambda b: (b, 0, 0))
    out_t = pl.BlockSpec((1, D_KV, n), lambda b: (b, 0, 0))
    wshape = (width, 2 * NSA_KV_HEADS * CMP_HIDDEN)
    return pl.pallas_call(
        _compress_kernel,
        grid=(batch,),
        in_specs=[blk, blk, const(wshape), const(wshape), const((8, width)), const((8, width)),
                  const((1, CMP_HIDDEN)), const((1, CMP_HIDDEN)),
                  const((CMP_HIDDEN, HEAD_DIM)), const((CMP_HIDDEN, HEAD_DIM))],
        out_specs=[out, out_t],
        out_shape=[jax.ShapeDtypeStruct((batch, n, D_KV), BF16),
                   jax.ShapeDtypeStruct((batch, D_KV, n), BF16)],
        compiler_params=pltpu.CompilerParams(
            dimension_semantics=("parallel",), vmem_limit_bytes=VMEM_LIMIT),
        name="compress_kv",
    )(k3, v3, _expand_w1(ck_w1), _expand_w1(cv_w1), _expand_pe(ck_pe), _expand_pe(cv_pe),
      ck_b1.reshape(1, CMP_HIDDEN), cv_b1.reshape(1, CMP_HIDDEN),
      ck_w2.astype(BF16), cv_w2.astype(BF16))


def _pad_group(x, g):
    z = jnp.zeros_like(x)
    return jnp.concatenate([x if gg == g else z for gg in range(NSA_KV_HEADS)], axis=0)


def _cmp_attn_kernel(qt_ref, kc_ref, vct_ref, ovt_ref, ocmpt_ref, selt_ref, score_ref, rank_ref):
    tq = qt_ref.shape[1]
    n_cmp = kc_ref.shape[1]
    n_blk = ovt_ref.shape[0]
    t0 = pl.program_id(1) * tq
    tq_pos = t0 + lax.broadcasted_iota(jnp.int32, (1, tq), 1)
    inv_gate = jnp.where(tq_pos >= CMP_LEN - 1, 1.0, 0.0)

    jj = lax.broadcasted_iota(jnp.int32, (n_blk, tq), 0)
    cur = jnp.right_shift(t0 + lax.broadcasted_iota(jnp.int32, (n_blk, tq), 1), SEL_SHIFT)
    valid = jj <= cur
    forced = (jj == 0) | (jj == cur) | (jj == cur - 1)
    sub8 = lax.broadcasted_iota(jnp.int32, (8, tq), 0)

    def attend(n_vis):
        kc = kc_ref[0, :n_vis, :]
        visible = (lax.broadcasted_iota(jnp.int32, (n_vis, tq), 0) * CMP_STRIDE
                   + (CMP_LEN - 1)) <= tq_pos

        def scores(h):
            return _dot(kc, _pad_group(qt_ref[h * HEAD_DIM:(h + 1) * HEAD_DIM, :], h // NSA_GROUP))

        s_next = scores(0)
        for h in range(NSA_HEADS):
            g, r = divmod(h, NSA_GROUP)
            s = jnp.where(visible, s_next, NEG)
            if h + 1 < NSA_HEADS:
                s_next = scores(h + 1)
            e = jnp.exp2(s - jnp.max(s, axis=0, keepdims=True))
            p = e * (inv_gate / jnp.sum(e, axis=0, keepdims=True))
            ocmpt_ref[h * HEAD_DIM:(h + 1) * HEAD_DIM, :] = _dot(
                vct_ref[0, g * HEAD_DIM:(g + 1) * HEAD_DIM, :n_vis], p.astype(BF16))
            psum = p if r == 0 else psum + p
            if r == NSA_GROUP - 1:
                hi = psum.astype(BF16)
                lo = (psum - hi.astype(F32)).astype(BF16)
                imp_t = _dot(ovt_ref[:, :n_vis], hi) + _dot(ovt_ref[:, :n_vis], lo)
                score_ref[g] = jnp.where(forced, -NEG, jnp.where(valid, imp_t, NEG))

    half = n_cmp // 2
    if half % 128 == 0:
        last_visible = (t0 + tq - CMP_LEN) // CMP_STRIDE
        lax.cond(last_visible < half, lambda: attend(half), lambda: attend(n_cmp))
    else:
        attend(n_cmp)

    n_grp = n_blk // 8
    rank_ref[...] = jnp.zeros(rank_ref.shape, F32)
    last_grp = jnp.right_shift(t0 + tq - 1, SEL_SHIFT + 3)

    def count(g, gi, gj):
        rows, blk = score_ref[g, 8 * gi:8 * gi + 8], score_ref[g, 8 * gj:8 * gj + 8]
        total = rank_ref[g, 8 * gj:8 * gj + 8]
        for k in range(8):
            row = jnp.broadcast_to(rows[k:k + 1, :], (8, tq))
            if gi < gj:
                before = jnp.where(row >= blk, 1.0, 0.0)
            elif gi > gj:
                before = jnp.where(row > blk, 1.0, 0.0)
            else:
                before = jnp.where(sub8 > k, jnp.where(row >= blk, 1.0, 0.0),
                                   jnp.where(row > blk, 1.0, 0.0))
            total = total + before
        rank_ref[g, 8 * gj:8 * gj + 8] = total

    for hi_grp in range(n_grp):
        @pl.when(last_grp >= hi_grp)
        def _(hi_grp=hi_grp):
            for g in range(NSA_KV_HEADS):
                for lo_grp in range(hi_grp):
                    count(g, hi_grp, lo_grp)
                    count(g, lo_grp, hi_grp)
                count(g, hi_grp, hi_grp)

    for g in range(NSA_KV_HEADS):
        selt_ref[g * n_blk:(g + 1) * n_blk, :] = jnp.where(
            rank_ref[g] < float(min(N_SELECT, n_blk)), 1.0, 0.0).astype(BF16)


def _overlap_t(n_cmp_pad, n_blk):
    st = np.arange(n_cmp_pad) * CMP_STRIDE
    bs = np.arange(n_blk) * SEL_BLOCK
    ov = np.minimum(st[:, None] + CMP_LEN, bs[None, :] + SEL_BLOCK) - np.maximum(st[:, None], bs[None, :])
    return jnp.asarray((np.clip(ov, 0, None) / CMP_LEN).T, dtype=BF16)


def _cmp_attn(qt, kc, vct, batch, seq):
    tq = CMP_Q_TILE
    n_cmp = kc.shape[1]
    n_blk = seq // SEL_BLOCK
    per = seq // tq
    cols = lambda height: pl.BlockSpec((height, tq), lambda b, j: (0, b * per + j))
    return pl.pallas_call(
        _cmp_attn_kernel,
        grid=(batch, per),
        in_specs=[
            cols(D_NSA),
            pl.BlockSpec((1, n_cmp, D_KV), lambda b, j: (b, 0, 0)),
            pl.BlockSpec((1, D_KV, n_cmp), lambda b, j: (b, 0, 0)),
            pl.BlockSpec((n_blk, n_cmp), lambda b, j: (0, 0)),
        ],
        out_specs=[cols(D_NSA), cols(NSA_KV_HEADS * n_blk)],
        out_shape=[jax.ShapeDtypeStruct((D_NSA, batch * seq), F32),
                   jax.ShapeDtypeStruct((NSA_KV_HEADS * n_blk, batch * seq), BF16)],
        scratch_shapes=[pltpu.VMEM((NSA_KV_HEADS, n_blk, tq), F32)] * 2,
        compiler_params=pltpu.CompilerParams(
            dimension_semantics=("parallel", "parallel"), vmem_limit_bytes=VMEM_LIMIT),
        name="cmp_attn_select",
    )(qt, kc, vct, _overlap_t(n_cmp, n_blk))


def _sel_win_kernel(qt_ref, kk_ref, vvt_ref, selt_ref, ocmpt_ref, gatet_ref, out_ref,
                    bias_ref, acc_ref, qaug_ref, sa_ref, sb_ref):
    qb = qt_ref.shape[1]
    n_blk = selt_ref.shape[0] // NSA_KV_HEADS
    kc = SEL_KEY_CHUNK
    lanes = NSA_GROUP * qb
    s0 = pl.program_id(1) * qb
    n_full = s0 // kc
    win_keys = WINDOW + qb
    win_start = pl.multiple_of(jnp.maximum(s0 - WINDOW, 0), qb)

    blk_per_chunk = kc // SEL_BLOCK
    bias = jnp.where(selt_ref[...].astype(F32) > 0.5, 0.0, NEG)
    if blk_per_chunk < 8:
        bias_ref[...] = jnp.zeros(bias_ref.shape, F32)
    for c in range(bias_ref.shape[0]):
        bias_ref[c, :blk_per_chunk, :] = bias[c * blk_per_chunk:(c + 1) * blk_per_chunk]
    tq = s0 + lax.broadcasted_iota(jnp.int32, (1, qb), 1)
    dist = tq - (win_start + lax.broadcasted_iota(jnp.int32, (win_keys, qb), 0))
    wbias = jnp.where(lax.bitcast_convert_type(dist, jnp.uint32) < WINDOW, 0.0, NEG)
    wbias = jnp.concatenate([wbias] * NSA_GROUP, axis=1)
    gt = gatet_ref[...]

    groups = range(NSA_KV_HEADS)
    qg = [_pad_group(jnp.concatenate(
        [qt_ref[h * HEAD_DIM:(h + 1) * HEAD_DIM, :]
         for h in range(g * NSA_GROUP, (g + 1) * NSA_GROUP)], axis=1), g) for g in groups]
    onehot = jnp.where(
        jnp.right_shift(lax.broadcasted_iota(jnp.int32, (kc, D_KV), 0), SEL_SHIFT)
        == lax.broadcasted_iota(jnp.int32, (kc, D_KV), 1), 1.0, 0.0).astype(BF16)
    for g in groups:
        qaug_ref[g, :D_KV, :] = qg[g]
        qaug_ref[g, D_KV:, :] = jnp.zeros((D_KV, lanes), BF16)

    def scores(g, k0):
        rows = jnp.concatenate([bias_ref[g * (n_blk // blk_per_chunk) + k0 // kc],
                                jnp.zeros((8, qb), F32)], axis=0).astype(BF16)
        qaug_ref[g, D_KV:D_KV + 16, :] = jnp.concatenate([rows] * NSA_GROUP, axis=1)
        if not isinstance(k0, int):
            k0 = pl.multiple_of(k0, kc)
        keys = jnp.concatenate([kk_ref[pl.ds(k0, kc), :D_KV], onehot], axis=1)
        return _dot(keys, qaug_ref[g])

    def update(g, k0, s, m, l):
        m_new = jnp.maximum(m, jnp.max(s, axis=0, keepdims=True))
        alpha = jnp.exp2(m - m_new)
        p = jnp.exp2(s - m_new)
        v = vvt_ref[g * HEAD_DIM:(g + 1) * HEAD_DIM, pl.ds(k0, kc)]
        acc_ref[g] = alpha * acc_ref[g] + _dot(v, p.astype(BF16))
        return m_new, alpha * l + jnp.sum(p, axis=0, keepdims=True)

    def score_into(ref, c):
        for g in groups:
            ref[g] = scores(g, c * kc)

    def softmax_from(ref, c, carry):
        k0 = pl.multiple_of(c * kc, kc)
        return tuple(update(g, k0, ref[g], *carry[g]) for g in groups)

    acc_ref[...] = jnp.zeros(acc_ref.shape, F32)
    init = ((jnp.full((1, lanes), NEG, F32), jnp.zeros((1, lanes), F32)),) * NSA_KV_HEADS
    odd = lax.rem(n_full, 2)

    def peel_odd():
        score_into(sb_ref, 0)
        score_into(sa_ref, 1)
        return softmax_from(sb_ref, 0, init)

    def peel_even():
        score_into(sa_ref, 0)
        return init

    carry = lax.cond(odd == 1, peel_odd, peel_even)

    def body(i, carry):
        c = 2 * i + odd
        score_into(sb_ref, c + 1)
        carry = softmax_from(sa_ref, c, carry)
        score_into(sa_ref, c + 2)
        return softmax_from(sb_ref, c + 1, carry)

    carry = lax.fori_loop(0, n_full // 2, body, carry)
    k_last = pl.multiple_of(n_full * kc, kc)
    seen = k_last + lax.broadcasted_iota(jnp.int32, (kc, qb), 0) <= tq

    def last_scores(g):
        s = sa_ref[g]
        return jnp.concatenate([jnp.where(seen, s[:, r * qb:(r + 1) * qb], NEG)
                                for r in range(NSA_GROUP)], axis=1)

    def window_scores(g):
        return _dot(kk_ref[pl.ds(win_start, win_keys), D_KV:], qg[g]) + wbias

    def window_out(g, sw):
        pw = jnp.exp2(sw - jnp.max(sw, axis=0, keepdims=True))
        vw = vvt_ref[D_KV + g * HEAD_DIM:D_KV + (g + 1) * HEAD_DIM, pl.ds(win_start, win_keys)]
        return _dot(vw, pw.astype(BF16)) * (1.0 / jnp.sum(pw, axis=0, keepdims=True))

    o_sel, o_win, s_win = [], [], [None] * NSA_KV_HEADS
    for g in groups:
        s_win[g] = window_scores(g)
        _, l = update(g, k_last, last_scores(g), *carry[g])
        o_sel.append(acc_ref[g] * (1.0 / l))
    for g in groups:
        o_win.append(window_out(g, s_win[g]))

    pieces = []
    for g in groups:
        o_s, o_w = o_sel[g], o_win[g]
        for r in range(NSA_GROUP):
            h = g * NSA_GROUP + r
            ln = slice(r * qb, (r + 1) * qb)
            pieces.append(gt[3 * h:3 * h + 1] * ocmpt_ref[h * HEAD_DIM:(h + 1) * HEAD_DIM, :]
                          + gt[3 * h + 1:3 * h + 2] * o_s[:, ln]
                          + gt[3 * h + 2:3 * h + 3] * o_w[:, ln])
    out_ref[...] = jnp.concatenate(pieces, axis=0).T.astype(BF16)


def _sel_win(qt, kk, vvt, selt, ocmpt, gatet, batch, seq):
    qb = Q_BLOCK
    per = seq // qb
    cols = lambda height: pl.BlockSpec((height, qb), lambda b, i: (0, b * per + i))
    return pl.pallas_call(
        _sel_win_kernel,
        grid=(batch, per),
        in_specs=[cols(D_NSA),
                  pl.BlockSpec((seq, 2 * D_KV), lambda b, i: (b, 0)),
                  pl.BlockSpec((2 * D_KV, seq), lambda b, i: (0, b)),
                  cols(selt.shape[0]), cols(D_NSA), cols(128)],
        out_specs=pl.BlockSpec((qb, D_NSA), lambda b, i: (b * per + i, 0)),
        out_shape=jax.ShapeDtypeStruct((batch * seq, D_NSA), BF16),
        scratch_shapes=[pltpu.VMEM((selt.shape[0] * SEL_BLOCK // SEL_KEY_CHUNK, 8, qb), F32),
                        pltpu.VMEM((NSA_KV_HEADS, HEAD_DIM, NSA_GROUP * qb), F32),
                        pltpu.VMEM((NSA_KV_HEADS, 2 * D_KV, NSA_GROUP * qb), BF16),
                        pltpu.VMEM((NSA_KV_HEADS, SEL_KEY_CHUNK, NSA_GROUP * qb), F32),
                        pltpu.VMEM((NSA_KV_HEADS, SEL_KEY_CHUNK, NSA_GROUP * qb), F32)],
        compiler_params=pltpu.CompilerParams(
            dimension_semantics=("parallel", "arbitrary"), vmem_limit_bytes=VMEM_LIMIT),
        name="sel_win_attn",
    )(qt, kk, vvt, selt, ocmpt, gatet)


def kernel(x, ffn1_gate, ffn1_up, ffn1_down, ln1_g, ln1_b, w_in, conv_w, gmlp_ln_g, gmlp_ln_b, gmlp_w, gmlp_b, ck_pe, ck_w1, ck_b1, ck_w2, cv_pe, cv_w1, cv_b1, cv_w2, w_out, ln2_g, ln2_b, ffn2_gate, ffn2_up, ffn2_down, ln3_g, ln3_b):
    batch, seq, d = x.shape
    assert d == D_MODEL and seq % max(ROW_TILE, FFN_ROW_TILE, CMP_Q_TILE, SEL_KEY_CHUNK) == 0
    h = x.reshape(batch * seq, d)
    for l in range(ffn1_gate.shape[0]):
        h = _ffn_ln(h, ffn1_gate[l], ffn1_up[l], ffn1_down[l], ln1_g[l], ln1_b[l])
        yab, qt, kcr, vcr, kk, vvt, gatet = _proj(h, seq, w_in[l], conv_w[l], gmlp_ln_g[l],
                                                  gmlp_ln_b[l], gmlp_w[l], gmlp_b[l])
        kc, vct = _compress(kcr, vcr, batch, seq, ck_pe[l], ck_w1[l], ck_b1[l], ck_w2[l],
                            cv_pe[l], cv_w1[l], cv_b1[l], cv_w2[l])
        ocmpt, selt = _cmp_attn(qt, kc, vct, batch, seq)
        yc = _sel_win(qt, kk, vvt, selt, ocmpt, gatet, batch, seq)
        h = _ffn_ln(h, ffn2_gate[l], ffn2_up[l], ffn2_down[l], ln3_g[l], ln3_b[l],
                    mix=(yab, yc, w_out[l], ln2_g[l], ln2_b[l]))
    return h.reshape(batch, seq, d)
```

```python
import functools

import numpy as np
import jax
import jax.numpy as jnp
from jax import lax
from jax.experimental import pallas as pl
from jax.experimental.pallas import tpu as pltpu

F32 = jnp.float32
BF16 = jnp.bfloat16

D_MODEL = 1024
D_FF = 2816
D_CONV = 256
CONV_WIDTH = 3
D_GMLP = 256
GMLP_HEADS = 4
GMLP_HEAD_DIM = D_GMLP // GMLP_HEADS
GMLP_CHUNK = 128
HEAD_DIM = 64
NSA_HEADS = 8
NSA_KV_HEADS = 2
NSA_GROUP = NSA_HEADS // NSA_KV_HEADS
D_NSA = NSA_HEADS * HEAD_DIM
D_KV = NSA_KV_HEADS * HEAD_DIM
CMP_LEN = 32
CMP_STRIDE = 16
CMP_HIDDEN = 128
SEL_BLOCK = 64
SEL_SHIFT = 6
N_SELECT = 16
WINDOW = 512
DEPTH = 2
ALPHA = (2 * DEPTH) ** 0.25
LN_EPS = 1e-5
NEG = -1e30
LOG2E = 1.4426950408889634
Q_SCALE = HEAD_DIM ** -0.5 * LOG2E

FF_CHUNK = 256
ROW_TILE = 1024
FFN_ROW_TILE = 1024
FFN_SUB_TILE = 512
CMP_Q_TILE = 1024
Q_BLOCK = 256
SEL_KEY_CHUNK = 512
VMEM_LIMIT_FFN = 56 * 1024 * 1024
VMEM_LIMIT = 48 * 1024 * 1024


def _dot(a, b):
    return jnp.dot(a, b, preferred_element_type=F32)


def _dot_nt(a, b):
    return lax.dot_general(a, b, (((1,), (1,)), ((), ())), preferred_element_type=F32)


def _layer_norm(y, g, b):
    mu = jnp.mean(y, axis=-1, keepdims=True)
    d = y - mu
    var = jnp.mean(d * d, axis=-1, keepdims=True)
    return d * lax.rsqrt(var + LN_EPS) * g + b


def _gelu_tanh(x):
    return 0.5 * x * (1.0 + jnp.tanh(0.7978845608028654 * (x + 0.044715 * (x * x * x))))


def _ffn_ln_kernel(with_mix, x_ref, *refs):
    if with_mix:
        yab_ref, yc_ref, wab_ref, wc_ref, g2_ref, b2_ref, *refs = refs
    wg_ref, wu_ref, wd_ref, g_ref, b_ref, o_ref, acc_ref = refs
    for t in range(x_ref.shape[0] // FFN_SUB_TILE):
        rows = slice(t * FFN_SUB_TILE, (t + 1) * FFN_SUB_TILE)
        x = x_ref[rows, :]
        if with_mix:
            mix = _dot(yab_ref[rows, :], wab_ref[...]) + _dot(yc_ref[rows, :], wc_ref[...])
            x = _layer_norm(ALPHA * x + mix, g2_ref[...], b2_ref[...])
        xb = x.astype(BF16)
        for c in range(D_FF // FF_CHUNK):
            cols = slice(c * FF_CHUNK, (c + 1) * FF_CHUNK)
            gate = _dot(xb, wg_ref[:, cols])
            up = _dot(xb, wu_ref[:, cols])
            h = (gate * jax.nn.sigmoid(gate) * up).astype(BF16)
            down = _dot(h, wd_ref[cols, :])
            if c == 0:
                acc_ref[rows, :] = down
            else:
                acc_ref[rows, :] += down
        y = ALPHA * x + 0.5 * acc_ref[rows, :]
        o_ref[rows, :] = _layer_norm(y, g_ref[...], b_ref[...])


def _ffn_ln(x2d, wg, wu, wd, g, b, mix=None):
    rows = x2d.shape[0]
    resident = functools.partial(pl.BlockSpec, pipeline_mode=pl.Buffered(1))
    row = lambda width: pl.BlockSpec((FFN_ROW_TILE, width), lambda i: (i, 0))
    vec = pl.BlockSpec((1, D_MODEL), lambda i: (0, 0))
    operands, specs = [x2d], [row(D_MODEL)]
    if mix is not None:
        yab, yc, w_out, g2, b2 = mix
        d_ab = yab.shape[1]
        w = w_out.astype(BF16)
        operands += [yab, yc, w[:d_ab], w[d_ab:], g2.reshape(1, D_MODEL), b2.reshape(1, D_MODEL)]
        specs += [row(d_ab), row(yc.shape[1]), resident((d_ab, D_MODEL), lambda i: (0, 0)),
                  resident((yc.shape[1], D_MODEL), lambda i: (0, 0)), vec, vec]
    operands += [wg.astype(BF16), wu.astype(BF16), wd.astype(BF16),
                 g.reshape(1, D_MODEL), b.reshape(1, D_MODEL)]
    specs += [resident((D_MODEL, D_FF), lambda i: (0, 0)), resident((D_MODEL, D_FF), lambda i: (0, 0)),
              resident((D_FF, D_MODEL), lambda i: (0, 0)), vec, vec]
    return pl.pallas_call(
        functools.partial(_ffn_ln_kernel, mix is not None),
        grid=(rows // FFN_ROW_TILE,),
        in_specs=specs,
        out_specs=row(D_MODEL),
        out_shape=jax.ShapeDtypeStruct((rows, D_MODEL), F32),
        scratch_shapes=[pltpu.VMEM((FFN_ROW_TILE, D_MODEL), F32)],
        compiler_params=pltpu.CompilerParams(
            dimension_semantics=("parallel",), vmem_limit_bytes=VMEM_LIMIT_FFN),
        name="mix_ffn_ln" if mix is not None else "ffn_ln",
    )(*operands)


def _proj_kernel(tiles_per_seq, x_ref, xh_ref, wconv_ref, wuv_ref, wq_ref, wkv_ref, wgl_ref,
                 convw_ref, glg_ref, glb_ref, gw_ref, gb_ref,
                 yab_ref, qt_ref, kc_ref, vc_ref, kk_ref, vvt_ref, gatet_ref, stage_ref):
    i = pl.program_id(0)
    tm = x_ref.shape[0]
    xb = x_ref[...].astype(BF16)

    hbc = _dot(xb, wconv_ref[...])
    hbc_halo = _dot(xh_ref[...].astype(BF16), wconv_ref[...])
    uv = _dot(xb, wuv_ref[...])

    z = hbc[:, 2 * D_CONV:] * hbc[:, :D_CONV]
    z_halo = hbc_halo[:, 2 * D_CONV:] * hbc_halo[:, :D_CONV]
    z_halo = jnp.where(i % tiles_per_seq == 0, 0.0, z_halo)
    row = lax.broadcasted_iota(jnp.int32, (tm, D_CONV), 0)
    z1 = jnp.where(row == 0, z_halo[7:8], pltpu.roll(z, 1, 0))
    z2 = jnp.where(row == 0, z_halo[6:7], jnp.where(row == 1, z_halo[7:8], pltpu.roll(z, 2, 0)))
    cw = convw_ref[...]
    y_a = hbc[:, D_CONV:2 * D_CONV] * (cw[0:1] * z2 + cw[1:2] * z1 + cw[2:3] * z)

    q = _dot(xb, wq_ref[...])

    zz = _gelu_tanh(uv)
    u = zz[:, :D_GMLP]
    v = _layer_norm(zz[:, D_GMLP:], glg_ref[...], glb_ref[...]).astype(BF16)
    pr = lax.broadcasted_iota(jnp.int32, (GMLP_CHUNK, GMLP_CHUNK), 0)
    pc = lax.broadcasted_iota(jnp.int32, (GMLP_CHUNK, GMLP_CHUNK), 1)
    w_tril = [jnp.where(pr >= pc, gw_ref[h], 0.0).astype(BF16) for h in range(GMLP_HEADS)]
    sv_chunks = []
    for c in range(tm // GMLP_CHUNK):
        vch = v[c * GMLP_CHUNK:(c + 1) * GMLP_CHUNK]
        parts = [_dot(w_tril[h], vch[:, h * GMLP_HEAD_DIM:(h + 1) * GMLP_HEAD_DIM])
                 for h in range(GMLP_HEADS)]
        sv_chunks.append(jnp.concatenate(parts, axis=1) + gb_ref[...])
    y_b = u * jnp.concatenate(sv_chunks, axis=0)

    kv = _dot(xb, wkv_ref[...])
    gate_logits = _dot(xb, wgl_ref[...])
    yab_ref[...] = jnp.concatenate([y_a, y_b], axis=1).astype(BF16)
    qt_ref[...] = (q * Q_SCALE).T.astype(BF16)
    stage_ref[0] = kv[:, :D_KV]
    stage_ref[1] = kv[:, D_KV:2 * D_KV]
    for which, out_ref in enumerate((kc_ref, vc_ref)):
        for off in range(CMP_STRIDE):
            out_ref[:, off * D_KV:(off + 1) * D_KV] = stage_ref[
                which, pl.ds(off, tm // CMP_STRIDE, stride=CMP_STRIDE), :].astype(BF16)
    kk_ref[...] = jnp.concatenate([kv[:, 2 * D_KV:3 * D_KV], kv[:, 4 * D_KV:5 * D_KV]], axis=1).astype(BF16)
    vvt_ref[...] = jnp.concatenate([kv[:, 3 * D_KV:4 * D_KV], kv[:, 5 * D_KV:]], axis=1).T.astype(BF16)
    gatet_ref[...] = jax.nn.sigmoid(gate_logits).T


def _proj(x2d, seq, w_in, conv_w, gln_g, gln_b, gmlp_w, gmlp_b):
    rows = x2d.shape[0]
    tm = ROW_TILE
    w = w_in.astype(BF16)
    o = 0
    wconv = w[:, o:o + 3 * D_CONV]; o += 3 * D_CONV
    wuv = w[:, o:o + 2 * D_GMLP]; o += 2 * D_GMLP
    wq = w[:, o:o + D_NSA]; o += D_NSA
    wkv = w[:, o:o + 6 * D_KV]; o += 6 * D_KV
    n_gate = 3 * NSA_HEADS
    wgl = jnp.pad(w[:, o:o + n_gate], ((0, 0), (0, 128 - n_gate)))
    gb_exp = jnp.repeat(gmlp_b.T, GMLP_HEAD_DIM, axis=1)
    const = lambda shape: pl.BlockSpec(shape, lambda i: (0,) * len(shape))
    row_spec = lambda width: pl.BlockSpec((tm, width), lambda i: (i, 0))
    col_spec = lambda height: pl.BlockSpec((height, tm), lambda i: (0, i))
    strided_spec = pl.BlockSpec((tm // CMP_STRIDE, CMP_STRIDE * D_KV), lambda i: (i, 0))
    return pl.pallas_call(
        functools.partial(_proj_kernel, seq // tm),
        grid=(rows // tm,),
        in_specs=[
            row_spec(D_MODEL),
            pl.BlockSpec((8, D_MODEL), lambda i: (jnp.maximum(i * (tm // 8) - 1, 0), 0)),
            const(wconv.shape), const(wuv.shape), const(wq.shape), const(wkv.shape), const(wgl.shape),
            const((CONV_WIDTH, D_CONV)), const((1, D_GMLP)), const((1, D_GMLP)),
            const((GMLP_HEADS, GMLP_CHUNK, GMLP_CHUNK)), const((GMLP_CHUNK, D_GMLP)),
        ],
        out_specs=[row_spec(D_CONV + D_GMLP), col_spec(D_NSA), strided_spec, strided_spec,
                   row_spec(2 * D_KV), col_spec(2 * D_KV), col_spec(128)],
        out_shape=[
            jax.ShapeDtypeStruct((rows, D_CONV + D_GMLP), BF16),
            jax.ShapeDtypeStruct((D_NSA, rows), BF16),
            jax.ShapeDtypeStruct((rows // CMP_STRIDE, CMP_STRIDE * D_KV), BF16),
            jax.ShapeDtypeStruct((rows // CMP_STRIDE, CMP_STRIDE * D_KV), BF16),
            jax.ShapeDtypeStruct((rows, 2 * D_KV), BF16),
            jax.ShapeDtypeStruct((2 * D_KV, rows), BF16),
            jax.ShapeDtypeStruct((128, rows), F32),
        ],
        scratch_shapes=[pltpu.VMEM((2, tm, D_KV), F32)],
        compiler_params=pltpu.CompilerParams(
            dimension_semantics=("parallel",), vmem_limit_bytes=VMEM_LIMIT),
        name="in_proj",
    )(x2d, x2d, wconv, wuv, wq, wkv, wgl, conv_w, gln_g.reshape(1, D_GMLP), gln_b.reshape(1, D_GMLP),
      gmlp_w, gb_exp)


def _compress_kernel(k_ref, v_ref, w1k_ref, w1v_ref, pek_ref, pev_ref, b1k_ref, b1v_ref,
                     w2k_ref, w2v_ref, kc_ref, vct_ref):
    def one(x2d, w1, pe2, b1, w2):
        n = x2d.shape[0]
        r = _dot(x2d, w1)
        cst = _dot(pe2, w1)
        outs = []
        for g in range(NSA_KV_HEADS):
            lo = slice(g * CMP_HIDDEN, (g + 1) * CMP_HIDDEN)
            hi = slice((NSA_KV_HEADS + g) * CMP_HIDDEN, (NSA_KV_HEADS + g + 1) * CMP_HIDDEN)
            first = r[:, lo] + cst[0:1, lo]
            second = r[:, hi] + cst[1:2, hi]
            h = _gelu_tanh(first + pltpu.roll(second, n - 1, 0) + b1)
            outs.append(_dot(h.astype(BF16), w2))
        return jnp.concatenate(outs, axis=1)

    kc_ref[0] = one(k_ref[0], w1k_ref[...], pek_ref[...], b1k_ref[...], w2k_ref[...]).astype(BF16)
    vct_ref[0] = one(v_ref[0], w1v_ref[...], pev_ref[...], b1v_ref[...], w2v_ref[...]).T.astype(BF16)


def _expand_w1(w1):
    half = CMP_LEN // 2
    eye = jnp.eye(NSA_KV_HEADS, dtype=w1.dtype)
    w = w1.reshape(2, half, HEAD_DIM, CMP_HIDDEN)
    e = jnp.einsum("aldf,hg->lhdagf", w, eye)
    return e.reshape(half * NSA_KV_HEADS * HEAD_DIM, 2 * NSA_KV_HEADS * CMP_HIDDEN).astype(BF16)


def _expand_pe(pe):
    half = CMP_LEN // 2
    p = pe.reshape(2, half, 1, HEAD_DIM)
    p = jnp.broadcast_to(p, (2, half, NSA_KV_HEADS, HEAD_DIM)).reshape(2, -1)
    return jnp.pad(p, ((0, 6), (0, 0))).astype(BF16)


def _compress(kc2d, vc2d, batch, seq, ck_pe, ck_w1, ck_b1, ck_w2, cv_pe, cv_w1, cv_b1, cv_w2):
    n = seq // CMP_STRIDE
    width = CMP_STRIDE * D_KV
    k3 = kc2d.reshape(batch, n, width)
    v3 = vc2d.reshape(batch, n, width)
    const = lambda shape: pl.BlockSpec(shape, lambda b: (0,) * len(shape))
    blk = pl.BlockSpec((1, n, width), lambda b: (b, 0, 0))
    out = pl.BlockSpec((1, n, D_KV), lambda b: (b, 0, 0))
    out_t = pl.BlockSpec((1, D_KV, n), lambda b: (b, 0, 0))
    wshape = (width, 2 * NSA_KV_HEADS * CMP_HIDDEN)
    return pl.pallas_call(
        _compress_kernel,
        grid=(batch,),
        in_specs=[blk, blk, const(wshape), const(wshape), const((8, width)), const((8, width)),
                  const((1, CMP_HIDDEN)), const((1, CMP_HIDDEN)),
                  const((CMP_HIDDEN, HEAD_DIM)), const((CMP_HIDDEN, HEAD_DIM))],
        out_specs=[out, out_t],
        out_shape=[jax.ShapeDtypeStruct((batch, n, D_KV), BF16),
                   jax.ShapeDtypeStruct((batch, D_KV, n), BF16)],
        compiler_params=pltpu.CompilerParams(
            dimension_semantics=("parallel",), vmem_limit_bytes=VMEM_LIMIT),
        name="compress_kv",
    )(k3, v3, _expand_w1(ck_w1), _expand_w1(cv_w1), _expand_pe(ck_pe), _expand_pe(cv_pe),
      ck_b1.reshape(1, CMP_HIDDEN), cv_b1.reshape(1, CMP_HIDDEN),
      ck_w2.astype(BF16), cv_w2.astype(BF16))


def _pad_group(x, g):
    z = jnp.zeros_like(x)
    return jnp.concatenate([x if gg == g else z for gg in range(NSA_KV_HEADS)], axis=0)


def _cmp_attn_kernel(qt_ref, kc_ref, vct_ref, ovt_ref, ocmpt_ref, selt_ref, score_ref, rank_ref):
    tq = qt_ref.shape[1]
    n_cmp = kc_ref.shape[1]
    n_blk = ovt_ref.shape[0]
    t0 = pl.program_id(1) * tq
    tq_pos = t0 + lax.broadcasted_iota(jnp.int32, (1, tq), 1)
    inv_gate = jnp.where(tq_pos >= CMP_LEN - 1, 1.0, 0.0)

    jj = lax.broadcasted_iota(jnp.int32, (n_blk, tq), 0)
    cur = jnp.right_shift(t0 + lax.broadcasted_iota(jnp.int32, (n_blk, tq), 1), SEL_SHIFT)
    valid = jj <= cur
    forced = (jj == 0) | (jj == cur) | (jj == cur - 1)
    sub8 = lax.broadcasted_iota(jnp.int32, (8, tq), 0)

    def attend(n_vis):
        kc = kc_ref[0, :n_vis, :]
        visible = (lax.broadcasted_iota(jnp.int32, (n_vis, tq), 0) * CMP_STRIDE
                   + (CMP_LEN - 1)) <= tq_pos

        def scores(h):
            return _dot(kc, _pad_group(qt_ref[h * HEAD_DIM:(h + 1) * HEAD_DIM, :], h // NSA_GROUP))

        s_next = scores(0)
        for h in range(NSA_HEADS):
            g, r = divmod(h, NSA_GROUP)
            s = jnp.where(visible, s_next, NEG)
            if h + 1 < NSA_HEADS:
                s_next = scores(h + 1)
            e = jnp.exp2(s - jnp.max(s, axis=0, keepdims=True))
            p = e * (inv_gate / jnp.sum(e, axis=0, keepdims=True))
            ocmpt_ref[h * HEAD_DIM:(h + 1) * HEAD_DIM, :] = _dot(
                vct_ref[0, g * HEAD_DIM:(g + 1) * HEAD_DIM, :n_vis], p.astype(BF16))
            psum = p if r == 0 else psum + p
            if r == NSA_GROUP - 1:
                hi = psum.astype(BF16)
                lo = (psum - hi.astype(F32)).astype(BF16)
                imp_t = _dot(ovt_ref[:, :n_vis], hi) + _dot(ovt_ref[:, :n_vis], lo)
                score_ref[g] = jnp.where(forced, -NEG, jnp.where(valid, imp_t, NEG))

    half = n_cmp // 2
    if half % 128 == 0:
        last_visible = (t0 + tq - CMP_LEN) // CMP_STRIDE
        lax.cond(last_visible < half, lambda: attend(half), lambda: attend(n_cmp))
    else:
        attend(n_cmp)

    n_grp = n_blk // 8
    rank_ref[...] = jnp.zeros(rank_ref.shape, F32)
    last_grp = jnp.right_shift(t0 + tq - 1, SEL_SHIFT + 3)

    def count(g, gi, gj):
        rows, blk = score_ref[g, 8 * gi:8 * gi + 8], score_ref[g, 8 * gj:8 * gj + 8]
        total = rank_ref[g, 8 * gj:8 * gj + 8]
        for k in range(8):
            row = jnp.broadcast_to(rows[k:k + 1, :], (8, tq))
            if gi < gj:
                before = jnp.where(row >= blk, 1.0, 0.0)
            elif gi > gj:
                before = jnp.where(row > blk, 1.0, 0.0)
            else:
                before = jnp.where(sub8 > k, jnp.where(row >= blk, 1.0, 0.0),
                                   jnp.where(row > blk, 1.0, 0.0))
            total = total + before
        rank_ref[g, 8 * gj:8 * gj + 8] = total

    for hi_grp in range(n_grp):
        @pl.when(last_grp >= hi_grp)
        def _(hi_grp=hi_grp):
            for g in range(NSA_KV_HEADS):
                for lo_grp in range(hi_grp):
                    count(g, hi_grp, lo_grp)
                    count(g, lo_grp, hi_grp)
                count(g, hi_grp, hi_grp)

    for g in range(NSA_KV_HEADS):
        selt_ref[g * n_blk:(g + 1) * n_blk, :] = jnp.where(
            rank_ref[g] < float(min(N_SELECT, n_blk)), 1.0, 0.0).astype(BF16)


def _overlap_t(n_cmp_pad, n_blk):
    st = np.arange(n_cmp_pad) * CMP_STRIDE
    bs = np.arange(n_blk) * SEL_BLOCK
    ov = np.minimum(st[:, None] + CMP_LEN, bs[None, :] + SEL_BLOCK) - np.maximum(st[:, None], bs[None, :])
    return jnp.asarray((np.clip(ov, 0, None) / CMP_LEN).T, dtype=BF16)


def _cmp_attn(qt, kc, vct, batch, seq):
    tq = CMP_Q_TILE
    n_cmp = kc.shape[1]
    n_blk = seq // SEL_BLOCK
    per = seq // tq
    cols = lambda height: pl.BlockSpec((height, tq), lambda b, j: (0, b * per + j))
    return pl.pallas_call(
        _cmp_attn_kernel,
        grid=(batch, per),
        in_specs=[
            cols(D_NSA),
            pl.BlockSpec((1, n_cmp, D_KV), lambda b, j: (b, 0, 0)),
            pl.BlockSpec((1, D_KV, n_cmp), lambda b, j: (b, 0, 0)),
            pl.BlockSpec((n_blk, n_cmp), lambda b, j: (0, 0)),
        ],
        out_specs=[cols(D_NSA), cols(NSA_KV_HEADS * n_blk)],
        out_shape=[jax.ShapeDtypeStruct((D_NSA, batch * seq), F32),
                   jax.ShapeDtypeStruct((NSA_KV_HEADS * n_blk, batch * seq), BF16)],
        scratch_shapes=[pltpu.VMEM((NSA_KV_HEADS, n_blk, tq), F32)] * 2,
        compiler_params=pltpu.CompilerParams(
            dimension_semantics=("parallel", "parallel"), vmem_limit_bytes=VMEM_LIMIT),
        name="cmp_attn_select",
    )(qt, kc, vct, _overlap_t(n_cmp, n_blk))


def _sel_win_kernel(qt_ref, kk_ref, vvt_ref, selt_ref, ocmpt_ref, gatet_ref, out_ref,
                    bias_ref, acc_ref, qaug_ref, sa_ref, sb_ref):
    qb = qt_ref.shape[1]
    n_blk = selt_ref.shape[0] // NSA_KV_HEADS
    kc = SEL_KEY_CHUNK
    lanes = NSA_GROUP * qb
    s0 = pl.program_id(1) * qb
    n_full = s0 // kc
    win_keys = WINDOW + qb
    win_start = pl.multiple_of(jnp.maximum(s0 - WINDOW, 0), qb)

    blk_per_chunk = kc // SEL_BLOCK
    bias = jnp.where(selt_ref[...].astype(F32) > 0.5, 0.0, NEG)
    if blk_per_chunk < 8:
        bias_ref[...] = jnp.zeros(bias_ref.shape, F32)
    for c in range(bias_ref.shape[0]):
        bias_ref[c, :blk_per_chunk, :] = bias[c * blk_per_chunk:(c + 1) * blk_per_chunk]
    tq = s0 + lax.broadcasted_iota(jnp.int32, (1, qb), 1)
    dist = tq - (win_start + lax.broadcasted_iota(jnp.int32, (win_keys, qb), 0))
    wbias = jnp.where(lax.bitcast_convert_type(dist, jnp.uint32) < WINDOW, 0.0, NEG)
    wbias = jnp.concatenate([wbias] * NSA_GROUP, axis=1)
    gt = gatet_ref[...]

    groups = range(NSA_KV_HEADS)
    qg = [_pad_group(jnp.concatenate(
        [qt_ref[h * HEAD_DIM:(h + 1) * HEAD_DIM, :]
         for h in range(g * NSA_GROUP, (g + 1) * NSA_GROUP)], axis=1), g) for g in groups]
    onehot = jnp.where(
        jnp.right_shift(lax.broadcasted_iota(jnp.int32, (kc, D_KV), 0), SEL_SHIFT)
        == lax.broadcasted_iota(jnp.int32, (kc, D_KV), 1), 1.0, 0.0).astype(BF16)
    for g in groups:
        qaug_ref[g, :D_KV, :] = qg[g]
        qaug_ref[g, D_KV:, :] = jnp.zeros((D_KV, lanes), BF16)

    def scores(g, k0):
        rows = jnp.concatenate([bias_ref[g * (n_blk // blk_per_chunk) + k0 // kc],
                                jnp.zeros((8, qb), F32)], axis=0).astype(BF16)
        qaug_ref[g, D_KV:D_KV + 16, :] = jnp.concatenate([rows] * NSA_GROUP, axis=1)
        if not isinstance(k0, int):
            k0 = pl.multiple_of(k0, kc)
        keys = jnp.concatenate([kk_ref[pl.ds(k0, kc), :D_KV], onehot], axis=1)
        return _dot(keys, qaug_ref[g])

    def update(g, k0, s, m, l):
        m_new = jnp.maximum(m, jnp.max(s, axis=0, keepdims=True))
        alpha = jnp.exp2(m - m_new)
        p = jnp.exp2(s - m_new)
        v = vvt_ref[g * HEAD_DIM:(g + 1) * HEAD_DIM, pl.ds(k0, kc)]
        acc_ref[g] = alpha * acc_ref[g] + _dot(v, p.astype(BF16))
        return m_new, alpha * l + jnp.sum(p, axis=0, keepdims=True)

    def score_into(ref, c):
        for g in groups:
            ref[g] = scores(g, c * kc)

    def softmax_from(ref, c, carry):
        k0 = pl.multiple_of(c * kc, kc)
        return tuple(update(g, k0, ref[g], *carry[g]) for g in groups)

    acc_ref[...] = jnp.zeros(acc_ref.shape, F32)
    init = ((jnp.full((1, lanes), NEG, F32), jnp.zeros((1, lanes), F32)),) * NSA_KV_HEADS
    odd = lax.rem(n_full, 2)

    def peel_odd():
        score_into(sb_ref, 0)
        score_into(sa_ref, 1)
        return softmax_from(sb_ref, 0, init)

    def peel_even():
        score_into(sa_ref, 0)
        return init

    carry = lax.cond(odd == 1, peel_odd, peel_even)

    def body(i, carry):
        c = 2 * i + odd
        score_into(sb_ref, c + 1)
        carry = softmax_from(sa_ref, c, carry)
        score_into(sa_ref, c + 2)
        return softmax_from(sb_ref, c + 1, carry)

    carry = lax.fori_loop(0, n_full // 2, body, carry)
    k_last = pl.multiple_of(n_full * kc, kc)
    seen = k_last + lax.broadcasted_iota(jnp.int32, (kc, qb), 0) <= tq

    def last_scores(g):
        s = sa_ref[g]
        return jnp.concatenate([jnp.where(seen, s[:, r * qb:(r + 1) * qb], NEG)
                                for r in range(NSA_GROUP)], axis=1)

    def window_scores(g):
        return _dot(kk_ref[pl.ds(win_start, win_keys), D_KV:], qg[g]) + wbias

    def window_out(g, sw):
        pw = jnp.exp2(sw - jnp.max(sw, axis=0, keepdims=True))
        vw = vvt_ref[D_KV + g * HEAD_DIM:D_KV + (g + 1) * HEAD_DIM, pl.ds(win_start, win_keys)]
        return _dot(vw, pw.astype(BF16)) * (1.0 / jnp.sum(pw, axis=0, keepdims=True))

    o_sel, o_win, s_win = [], [], [None] * NSA_KV_HEADS
    for g in groups:
        s_win[g] = window_scores(g)
        _, l = update(g, k_last, last_scores(g), *carry[g])
        o_sel.append(acc_ref[g] * (1.0 / l))
    for g in groups:
        o_win.append(window_out(g, s_win[g]))

    pieces = []
    for g in groups:
        o_s, o_w = o_sel[g], o_win[g]
        for r in range(NSA_GROUP):
            h = g * NSA_GROUP + r
            ln = slice(r * qb, (r + 1) * qb)
            pieces.append(gt[3 * h:3 * h + 1] * ocmpt_ref[h * HEAD_DIM:(h + 1) * HEAD_DIM, :]
                          + gt[3 * h + 1:3 * h + 2] * o_s[:, ln]
                          + gt[3 * h + 2:3 * h + 3] * o_w[:, ln])
    out_ref[...] = jnp.concatenate(pieces, axis=0).T.astype(BF16)


def _sel_win(qt, kk, vvt, selt, ocmpt, gatet, batch, seq):
    qb = Q_BLOCK
    per = seq // qb
    cols = lambda height: pl.BlockSpec((height, qb), lambda b, i: (0, b * per + i))
    return pl.pallas_call(
        _sel_win_kernel,
        grid=(batch, per),
        in_specs=[cols(D_NSA),
                  pl.BlockSpec((seq, 2 * D_KV), lambda b, i: (b, 0)),
                  pl.BlockSpec((2 * D_KV, seq), lambda b, i: (0, b)),
                  cols(selt.shape[0]), cols(D_NSA), cols(128)],
        out_specs=pl.BlockSpec((qb, D_NSA), lambda b, i: (b * per + i, 0)),
        out_shape=jax.ShapeDtypeStruct((batch * seq, D_NSA), BF16),
        scratch_shapes=[pltpu.VMEM((selt.shape[0] * SEL_BLOCK // SEL_KEY_CHUNK, 8, qb), F32),
                        pltpu.VMEM((NSA_KV_HEADS, HEAD_DIM, NSA_GROUP * qb), F32),
                        pltpu.VMEM((NSA_KV_HEADS, 2 * D_KV, NSA_GROUP * qb), BF16),
                        pltpu.VMEM((NSA_KV_HEADS, SEL_KEY_CHUNK, NSA_GROUP * qb), F32),
                        pltpu.VMEM((NSA_KV_HEADS, SEL_KEY_CHUNK, NSA_GROUP * qb), F32)],
        compiler_params=pltpu.CompilerParams(
            dimension_semantics=("parallel", "arbitrary"), vmem_limit_bytes=VMEM_LIMIT),
        name="sel_win_attn",
    )(qt, kk, vvt, selt, ocmpt, gatet)


def kernel(x, ffn1_gate, ffn1_up, ffn1_down, ln1_g, ln1_b, w_in, conv_w, gmlp_ln_g, gmlp_ln_b, gmlp_w, gmlp_b, ck_pe, ck_w1, ck_b1, ck_w2, cv_pe, cv_w1, cv_b1, cv_w2, w_out, ln2_g, ln2_b, ffn2_gate, ffn2_up, ffn2_down, ln3_g, ln3_b):
    batch, seq, d = x.shape
    assert d == D_MODEL and seq % max(ROW_TILE, FFN_ROW_TILE, CMP_Q_TILE, SEL_KEY_CHUNK) == 0
    h = x.reshape(batch * seq, d)
    for l in range(ffn1_gate.shape[0]):
        h = _ffn_ln(h, ffn1_gate[l], ffn1_up[l], ffn1_down[l], ln1_g[l], ln1_b[l])
        yab, qt, kcr, vcr, kk, vvt, gatet = _proj(h, seq, w_in[l], conv_w[l], gmlp_ln_g[l],
                                                  gmlp_ln_b[l], gmlp_w[l], gmlp_b[l])
        kc, vct = _compress(kcr, vcr, batch, seq, ck_pe[l], ck_w1[l], ck_b1[l], ck_w2[l],
                            cv_pe[l], cv_w1[l], cv_b1[l], cv_w2[l])
        ocmpt, selt = _cmp_attn(qt, kc, vct, batch, seq)
        yc = _sel_win(qt, kk, vvt, selt, ocmpt, gatet, batch, seq)
        h = _ffn_ln(h, ffn2_gate[l], ffn2_up[l], ffn2_down[l], ln3_g[l], ln3_b[l],
                    mix=(yab, yc, w_out[l], ln2_g[l], ln2_b[l]))
    return h.reshape(batch, seq, d)
```

```python
import functools

import numpy as np
import jax
import jax.numpy as jnp
from jax import lax
from jax.experimental import pallas as pl
from jax.experimental.pallas import tpu as pltpu

F32 = jnp.float32
BF16 = jnp.bfloat16

D_MODEL = 1024
D_FF = 2816
D_CONV = 256
CONV_WIDTH = 3
D_GMLP = 256
GMLP_HEADS = 4
GMLP_HEAD_DIM = D_GMLP // GMLP_HEADS
GMLP_CHUNK = 128
HEAD_DIM = 64
NSA_HEADS = 8
NSA_KV_HEADS = 2
NSA_GROUP = NSA_HEADS // NSA_KV_HEADS
D_NSA = NSA_HEADS * HEAD_DIM
D_KV = NSA_KV_HEADS * HEAD_DIM
CMP_LEN = 32
CMP_STRIDE = 16
CMP_SHIFT = 4
CMP_HIDDEN = 128
SEL_BLOCK = 64
SEL_SHIFT = 6
N_SELECT = 16
WINDOW = 512
DEPTH = 2
ALPHA = (2 * DEPTH) ** 0.25
LN_EPS = 1e-5
NEG = -1e30
LOG2E = 1.4426950408889634
Q_SCALE = HEAD_DIM ** -0.5 * LOG2E

FF_CHUNK = 256
ROW_TILE = 1024
FFN_ROW_TILE = 1024
FFN_SUB_TILE = 512
CMP_Q_TILE = 1024
Q_BLOCK = 256
SEL_KEY_CHUNK = 512
VMEM_LIMIT_FFN = 56 * 1024 * 1024
VMEM_LIMIT = 48 * 1024 * 1024


def _dot(a, b):
    return jnp.dot(a, b, preferred_element_type=F32)


def _dot_nt(a, b):
    return lax.dot_general(a, b, (((1,), (1,)), ((), ())), preferred_element_type=F32)


def _layer_norm(y, g, b):
    mu = jnp.mean(y, axis=-1, keepdims=True)
    d = y - mu
    var = jnp.mean(d * d, axis=-1, keepdims=True)
    return d * lax.rsqrt(var + LN_EPS) * g + b


def _gelu_tanh(x):
    return 0.5 * x * (1.0 + jnp.tanh(0.7978845608028654 * (x + 0.044715 * (x * x * x))))


def _ffn_ln_kernel(with_mix, x_ref, *refs):
    if with_mix:
        yab_ref, yc_ref, wab_ref, wc_ref, g2_ref, b2_ref, *refs = refs
    wg_ref, wu_ref, wd_ref, g_ref, b_ref, o_ref, acc_ref = refs
    for t in range(x_ref.shape[0] // FFN_SUB_TILE):
        rows = slice(t * FFN_SUB_TILE, (t + 1) * FFN_SUB_TILE)
        x = x_ref[rows, :]
        if with_mix:
            mix = _dot(yab_ref[rows, :], wab_ref[...]) + _dot(yc_ref[rows, :], wc_ref[...])
            x = _layer_norm(ALPHA * x + mix, g2_ref[...], b2_ref[...])
        xb = x.astype(BF16)
        for c in range(D_FF // FF_CHUNK):
            cols = slice(c * FF_CHUNK, (c + 1) * FF_CHUNK)
            gate = _dot(xb, wg_ref[:, cols])
            up = _dot(xb, wu_ref[:, cols])
            h = (gate * jax.nn.sigmoid(gate) * up).astype(BF16)
            down = _dot(h, wd_ref[cols, :])
            if c == 0:
                acc_ref[rows, :] = down
            else:
                acc_ref[rows, :] += down
        y = ALPHA * x + 0.5 * acc_ref[rows, :]
        o_ref[rows, :] = _layer_norm(y, g_ref[...], b_ref[...])


def _ffn_ln(x2d, wg, wu, wd, g, b, mix=None):
    rows = x2d.shape[0]
    resident = functools.partial(pl.BlockSpec, pipeline_mode=pl.Buffered(1))
    row = lambda width: pl.BlockSpec((FFN_ROW_TILE, width), lambda i: (i, 0))
    vec = pl.BlockSpec((1, D_MODEL), lambda i: (0, 0))
    operands, specs = [x2d], [row(D_MODEL)]
    if mix is not None:
        yab, yc, w_out, g2, b2 = mix
        d_ab = yab.shape[1]
        w = w_out.astype(BF16)
        operands += [yab, yc, w[:d_ab], w[d_ab:], g2.reshape(1, D_MODEL), b2.reshape(1, D_MODEL)]
        specs += [row(d_ab), row(yc.shape[1]), resident((d_ab, D_MODEL), lambda i: (0, 0)),
                  resident((yc.shape[1], D_MODEL), lambda i: (0, 0)), vec, vec]
    operands += [wg.astype(BF16), wu.astype(BF16), wd.astype(BF16),
                 g.reshape(1, D_MODEL), b.reshape(1, D_MODEL)]
    specs += [resident((D_MODEL, D_FF), lambda i: (0, 0)), resident((D_MODEL, D_FF), lambda i: (0, 0)),
              resident((D_FF, D_MODEL), lambda i: (0, 0)), vec, vec]
    return pl.pallas_call(
        functools.partial(_ffn_ln_kernel, mix is not None),
        grid=(rows // FFN_ROW_TILE,),
        in_specs=specs,
        out_specs=row(D_MODEL),
        out_shape=jax.ShapeDtypeStruct((rows, D_MODEL), F32),
        scratch_shapes=[pltpu.VMEM((FFN_ROW_TILE, D_MODEL), F32)],
        compiler_params=pltpu.CompilerParams(
            dimension_semantics=("parallel",), vmem_limit_bytes=VMEM_LIMIT_FFN),
        name="mix_ffn_ln" if mix is not None else "ffn_ln",
    )(*operands)


def _proj_kernel(tiles_per_seq, x_ref, xh_ref, wconv_ref, wuv_ref, wq_ref, wkv_ref, wgl_ref,
                 convw_ref, glg_ref, glb_ref, gw_ref, gb_ref,
                 yab_ref, qt_ref, kc_ref, vc_ref, kk_ref, vvt_ref, gatet_ref, stage_ref):
    i = pl.program_id(0)
    tm = x_ref.shape[0]
    xb = x_ref[...].astype(BF16)

    hbc = _dot(xb, wconv_ref[...])
    hbc_halo = _dot(xh_ref[...].astype(BF16), wconv_ref[...])
    uv = _dot(xb, wuv_ref[...])

    z = hbc[:, 2 * D_CONV:] * hbc[:, :D_CONV]
    z_halo = hbc_halo[:, 2 * D_CONV:] * hbc_halo[:, :D_CONV]
    z_halo = jnp.where(i % tiles_per_seq == 0, 0.0, z_halo)
    row = lax.broadcasted_iota(jnp.int32, (tm, D_CONV), 0)
    z1 = jnp.where(row == 0, z_halo[7:8], pltpu.roll(z, 1, 0))
    z2 = jnp.where(row == 0, z_halo[6:7], jnp.where(row == 1, z_halo[7:8], pltpu.roll(z, 2, 0)))
    cw = convw_ref[...]
    y_a = hbc[:, D_CONV:2 * D_CONV] * (cw[0:1] * z2 + cw[1:2] * z1 + cw[2:3] * z)

    q = _dot(xb, wq_ref[...])

    zz = _gelu_tanh(uv)
    u = zz[:, :D_GMLP]
    v = _layer_norm(zz[:, D_GMLP:], glg_ref[...], glb_ref[...]).astype(BF16)
    pr = lax.broadcasted_iota(jnp.int32, (GMLP_CHUNK, GMLP_CHUNK), 0)
    pc = lax.broadcasted_iota(jnp.int32, (GMLP_CHUNK, GMLP_CHUNK), 1)
    w_tril = [jnp.where(pr >= pc, gw_ref[h], 0.0).astype(BF16) for h in range(GMLP_HEADS)]
    sv_chunks = []
    for c in range(tm // GMLP_CHUNK):
        vch = v[c * GMLP_CHUNK:(c + 1) * GMLP_CHUNK]
        parts = [_dot(w_tril[h], vch[:, h * GMLP_HEAD_DIM:(h + 1) * GMLP_HEAD_DIM])
                 for h in range(GMLP_HEADS)]
        sv_chunks.append(jnp.concatenate(parts, axis=1) + gb_ref[...])
    y_b = u * jnp.concatenate(sv_chunks, axis=0)

    kv = _dot(xb, wkv_ref[...])
    gate_logits = _dot(xb, wgl_ref[...])
    yab_ref[...] = jnp.concatenate([y_a, y_b], axis=1).astype(BF16)
    qt_ref[...] = (q * Q_SCALE).T.astype(BF16)
    stage_ref[0] = kv[:, :D_KV]
    stage_ref[1] = kv[:, D_KV:2 * D_KV]
    for which, out_ref in enumerate((kc_ref, vc_ref)):
        for off in range(CMP_STRIDE):
            out_ref[:, off * D_KV:(off + 1) * D_KV] = stage_ref[
                which, pl.ds(off, tm // CMP_STRIDE, stride=CMP_STRIDE), :].astype(BF16)
    kk_ref[...] = jnp.concatenate([kv[:, 2 * D_KV:3 * D_KV], kv[:, 4 * D_KV:5 * D_KV]], axis=1).astype(BF16)
    vvt_ref[...] = jnp.concatenate([kv[:, 3 * D_KV:4 * D_KV], kv[:, 5 * D_KV:]], axis=1).T.astype(BF16)
    gatet_ref[...] = jax.nn.sigmoid(gate_logits).T


def _proj(x2d, seq, w_in, conv_w, gln_g, gln_b, gmlp_w, gmlp_b):
    rows = x2d.shape[0]
    tm = ROW_TILE
    w = w_in.astype(BF16)
    o = 0
    wconv = w[:, o:o + 3 * D_CONV]; o += 3 * D_CONV
    wuv = w[:, o:o + 2 * D_GMLP]; o += 2 * D_GMLP
    wq = w[:, o:o + D_NSA]; o += D_NSA
    wkv = w[:, o:o + 6 * D_KV]; o += 6 * D_KV
    n_gate = 3 * NSA_HEADS
    wgl = jnp.pad(w[:, o:o + n_gate], ((0, 0), (0, 128 - n_gate)))
    gb_exp = jnp.repeat(gmlp_b.T, GMLP_HEAD_DIM, axis=1)
    const = lambda shape: pl.BlockSpec(shape, lambda i: (0,) * len(shape))
    row_spec = lambda width: pl.BlockSpec((tm, width), lambda i: (i, 0))
    col_spec = lambda height: pl.BlockSpec((height, tm), lambda i: (0, i))
    strided_spec = pl.BlockSpec((tm // CMP_STRIDE, CMP_STRIDE * D_KV), lambda i: (i, 0))
    return pl.pallas_call(
        functools.partial(_proj_kernel, seq // tm),
        grid=(rows // tm,),
        in_specs=[
            row_spec(D_MODEL),
            pl.BlockSpec((8, D_MODEL), lambda i: (jnp.maximum(i * (tm // 8) - 1, 0), 0)),
            const(wconv.shape), const(wuv.shape), const(wq.shape), const(wkv.shape), const(wgl.shape),
            const((CONV_WIDTH, D_CONV)), const((1, D_GMLP)), const((1, D_GMLP)),
            const((GMLP_HEADS, GMLP_CHUNK, GMLP_CHUNK)), const((GMLP_CHUNK, D_GMLP)),
        ],
        out_specs=[row_spec(D_CONV + D_GMLP), col_spec(D_NSA), strided_spec, strided_spec,
                   row_spec(2 * D_KV), col_spec(2 * D_KV), col_spec(128)],
        out_shape=[
            jax.ShapeDtypeStruct((rows, D_CONV + D_GMLP), BF16),
            jax.ShapeDtypeStruct((D_NSA, rows), BF16),
            jax.ShapeDtypeStruct((rows // CMP_STRIDE, CMP_STRIDE * D_KV), BF16),
            jax.ShapeDtypeStruct((rows // CMP_STRIDE, CMP_STRIDE * D_KV), BF16),
            jax.ShapeDtypeStruct((rows, 2 * D_KV), BF16),
            jax.ShapeDtypeStruct((2 * D_KV, rows), BF16),
            jax.ShapeDtypeStruct((128, rows), F32),
        ],
        scratch_shapes=[pltpu.VMEM((2, tm, D_KV), F32)],
        compiler_params=pltpu.CompilerParams(
            dimension_semantics=("parallel",), vmem_limit_bytes=VMEM_LIMIT),
        name="in_proj",
    )(x2d, x2d, wconv, wuv, wq, wkv, wgl, conv_w, gln_g.reshape(1, D_GMLP), gln_b.reshape(1, D_GMLP),
      gmlp_w, gb_exp)


def _compress_kernel(k_ref, v_ref, w1k_ref, w1v_ref, pek_ref, pev_ref, b1k_ref, b1v_ref,
                     w2k_ref, w2v_ref, kc_ref, vct_ref):
    def one(x2d, w1, pe2, b1, w2):
        n = x2d.shape[0]
        r = _dot(x2d, w1)
        cst = _dot(pe2, w1)
        outs = []
        for g in range(NSA_KV_HEADS):
            lo = slice(g * CMP_HIDDEN, (g + 1) * CMP_HIDDEN)
            hi = slice((NSA_KV_HEADS + g) * CMP_HIDDEN, (NSA_KV_HEADS + g + 1) * CMP_HIDDEN)
            first = r[:, lo] + cst[0:1, lo]
            second = r[:, hi] + cst[1:2, hi]
            h = _gelu_tanh(first + pltpu.roll(second, n - 1, 0) + b1)
            outs.append(_dot(h.astype(BF16), w2))
        return jnp.concatenate(outs, axis=1)

    kc_ref[0] = one(k_ref[0], w1k_ref[...], pek_ref[...], b1k_ref[...], w2k_ref[...]).astype(BF16)
    vct_ref[0] = one(v_ref[0], w1v_ref[...], pev_ref[...], b1v_ref[...], w2v_ref[...]).T.astype(BF16)


def _expand_w1(w1):
    half = CMP_LEN // 2
    eye = jnp.eye(NSA_KV_HEADS, dtype=w1.dtype)
    w = w1.reshape(2, half, HEAD_DIM, CMP_HIDDEN)
    e = jnp.einsum("aldf,hg->lhdagf", w, eye)
    return e.reshape(half * NSA_KV_HEADS * HEAD_DIM, 2 * NSA_KV_HEADS * CMP_HIDDEN).astype(BF16)


def _expand_pe(pe):
    half = CMP_LEN // 2
    p = pe.reshape(2, half, 1, HEAD_DIM)
    p = jnp.broadcast_to(p, (2, half, NSA_KV_HEADS, HEAD_DIM)).reshape(2, -1)
    return jnp.pad(p, ((0, 6), (0, 0))).astype(BF16)


def _compress(kc2d, vc2d, batch, seq, ck_pe, ck_w1, ck_b1, ck_w2, cv_pe, cv_w1, cv_b1, cv_w2):
    n = seq // CMP_STRIDE
    width = CMP_STRIDE * D_KV
    k3 = kc2d.reshape(batch, n, width)
    v3 = vc2d.reshape(batch, n, width)
    const = lambda shape: pl.BlockSpec(shape, lambda b: (0,) * len(shape))
    blk = pl.BlockSpec((1, n, width), lambda b: (b, 0, 0))
    out = pl.BlockSpec((1, n, D_KV), lambda b: (b, 0, 0))
    out_t = pl.BlockSpec((1, D_KV, n), lambda b: (b, 0, 0))
    wshape = (width, 2 * NSA_KV_HEADS * CMP_HIDDEN)
    return pl.pallas_call(
        _compress_kernel,
        grid=(batch,),
        in_specs=[blk, blk, const(wshape), const(wshape), const((8, width)), const((8, width)),
                  const((1, CMP_HIDDEN)), const((1, CMP_HIDDEN)),
                  const((CMP_HIDDEN, HEAD_DIM)), const((CMP_HIDDEN, HEAD_DIM))],
        out_specs=[out, out_t],
        out_shape=[jax.ShapeDtypeStruct((batch, n, D_KV), BF16),
                   jax.ShapeDtypeStruct((batch, D_KV, n), BF16)],
        compiler_params=pltpu.CompilerParams(
            dimension_semantics=("parallel",), vmem_limit_bytes=VMEM_LIMIT),
        name="compress_kv",
    )(k3, v3, _expand_w1(ck_w1), _expand_w1(cv_w1), _expand_pe(ck_pe), _expand_pe(cv_pe),
      ck_b1.reshape(1, CMP_HIDDEN), cv_b1.reshape(1, CMP_HIDDEN),
      ck_w2.astype(BF16), cv_w2.astype(BF16))


def _pad_group(x, g):
    z = jnp.zeros_like(x)
    return jnp.concatenate([x if gg == g else z for gg in range(NSA_KV_HEADS)], axis=0)


def _cmp_attn_kernel(qt_ref, kc_ref, vct_ref, ovt_ref, ocmpt_ref, selt_ref, score_ref, rank_ref):
    tq = qt_ref.shape[1]
    n_cmp = kc_ref.shape[1]
    n_blk = ovt_ref.shape[0]
    t0 = pl.program_id(1) * tq
    tq_pos = t0 + lax.broadcasted_iota(jnp.int32, (1, tq), 1)
    inv_gate = jnp.where(tq_pos >= CMP_LEN - 1, 1.0, 0.0)

    jj = lax.broadcasted_iota(jnp.int32, (n_blk, tq), 0)
    cur = jnp.right_shift(t0 + lax.broadcasted_iota(jnp.int32, (n_blk, tq), 1), SEL_SHIFT)
    valid = jj <= cur
    forced = (jj == 0) | (jj == cur) | (jj == cur - 1)
    sub8 = lax.broadcasted_iota(jnp.int32, (8, tq), 0)

    def attend(n_vis):
        jbase = jnp.right_shift(t0 - (CMP_LEN - 1), CMP_SHIFT)
        key_idx = lax.broadcasted_iota(jnp.int32, (n_vis, D_KV), 0)
        blk_col = lax.broadcasted_iota(jnp.int32, (n_vis, D_KV), 1)
        keys = jnp.concatenate(
            [kc_ref[0, :n_vis, :],
             jnp.where(key_idx > jbase + blk_col, NEG, 0.0).astype(BF16)], axis=1)
        q_blk = jnp.right_shift(tq_pos - (CMP_LEN - 1), CMP_SHIFT) - jbase
        onehot = jnp.where(lax.broadcasted_iota(jnp.int32, (D_KV, tq), 0) == q_blk,
                           1.0, 0.0).astype(BF16)

        def scores(h):
            qh = _pad_group(qt_ref[h * HEAD_DIM:(h + 1) * HEAD_DIM, :], h // NSA_GROUP)
            return _dot(keys, jnp.concatenate([qh, onehot], axis=0))

        s_next = scores(0)
        for h in range(NSA_HEADS):
            g, r = divmod(h, NSA_GROUP)
            s = s_next
            if h + 1 < NSA_HEADS:
                s_next = scores(h + 1)
            e = jnp.exp2(s - jnp.max(s, axis=0, keepdims=True))
            p = e * (inv_gate / jnp.sum(e, axis=0, keepdims=True))
            ocmpt_ref[h * HEAD_DIM:(h + 1) * HEAD_DIM, :] = _dot(
                vct_ref[0, g * HEAD_DIM:(g + 1) * HEAD_DIM, :n_vis], p.astype(BF16))
            psum = p if r == 0 else psum + p
            if r == NSA_GROUP - 1:
                hi = psum.astype(BF16)
                lo = (psum - hi.astype(F32)).astype(BF16)
                imp_t = _dot(ovt_ref[:, :n_vis], hi) + _dot(ovt_ref[:, :n_vis], lo)
                score_ref[g] = jnp.where(forced, -NEG, jnp.where(valid, imp_t, NEG))

    half = n_cmp // 2
    if half % 128 == 0:
        last_visible = (t0 + tq - CMP_LEN) // CMP_STRIDE
        lax.cond(last_visible < half, lambda: attend(half), lambda: attend(n_cmp))
    else:
        attend(n_cmp)

    n_grp = n_blk // 8
    rank_ref[...] = jnp.zeros(rank_ref.shape, F32)
    last_grp = jnp.right_shift(t0 + tq - 1, SEL_SHIFT + 3)

    def count(g, gi, gj):
        rows, blk = score_ref[g, 8 * gi:8 * gi + 8], score_ref[g, 8 * gj:8 * gj + 8]
        total = rank_ref[g, 8 * gj:8 * gj + 8]
        for k in range(8):
            row = jnp.broadcast_to(rows[k:k + 1, :], (8, tq))
            if gi < gj:
                before = jnp.where(row >= blk, 1.0, 0.0)
            elif gi > gj:
                before = jnp.where(row > blk, 1.0, 0.0)
            else:
                before = jnp.where(sub8 > k, jnp.where(row >= blk, 1.0, 0.0),
                                   jnp.where(row > blk, 1.0, 0.0))
            total = total + before
        rank_ref[g, 8 * gj:8 * gj + 8] = total

    for hi_grp in range(n_grp):
        @pl.when(last_grp >= hi_grp)
        def _(hi_grp=hi_grp):
            for g in range(NSA_KV_HEADS):
                for lo_grp in range(hi_grp):
                    count(g, hi_grp, lo_grp)
                    count(g, lo_grp, hi_grp)
                count(g, hi_grp, hi_grp)

    for g in range(NSA_KV_HEADS):
        selt_ref[g * n_blk:(g + 1) * n_blk, :] = jnp.where(
            rank_ref[g] < float(min(N_SELECT, n_blk)), 1.0, 0.0).astype(BF16)


def _overlap_t(n_cmp_pad, n_blk):
    st = np.arange(n_cmp_pad) * CMP_STRIDE
    bs = np.arange(n_blk) * SEL_BLOCK
    ov = np.minimum(st[:, None] + CMP_LEN, bs[None, :] + SEL_BLOCK) - np.maximum(st[:, None], bs[None, :])
    return jnp.asarray((np.clip(ov, 0, None) / CMP_LEN).T, dtype=BF16)


def _cmp_attn(qt, kc, vct, batch, seq):
    tq = CMP_Q_TILE
    assert tq // CMP_STRIDE + 2 <= D_KV
    n_cmp = kc.shape[1]
    n_blk = seq // SEL_BLOCK
    per = seq // tq
    cols = lambda height: pl.BlockSpec((height, tq), lambda b, j: (0, b * per + j))
    return pl.pallas_call(
        _cmp_attn_kernel,
        grid=(batch, per),
        in_specs=[
            cols(D_NSA),
            pl.BlockSpec((1, n_cmp, D_KV), lambda b, j: (b, 0, 0)),
            pl.BlockSpec((1, D_KV, n_cmp), lambda b, j: (b, 0, 0)),
            pl.BlockSpec((n_blk, n_cmp), lambda b, j: (0, 0)),
        ],
        out_specs=[cols(D_NSA), cols(NSA_KV_HEADS * n_blk)],
        out_shape=[jax.ShapeDtypeStruct((D_NSA, batch * seq), F32),
                   jax.ShapeDtypeStruct((NSA_KV_HEADS * n_blk, batch * seq), BF16)],
        scratch_shapes=[pltpu.VMEM((NSA_KV_HEADS, n_blk, tq), F32)] * 2,
        compiler_params=pltpu.CompilerParams(
            dimension_semantics=("parallel", "parallel"), vmem_limit_bytes=VMEM_LIMIT),
        name="cmp_attn_select",
    )(qt, kc, vct, _overlap_t(n_cmp, n_blk))


def _sel_win_kernel(qt_ref, kk_ref, vvt_ref, selt_ref, ocmpt_ref, gatet_ref, out_ref,
                    bias_ref, acc_ref, qaug_ref, sa_ref, sb_ref):
    qb = qt_ref.shape[1]
    n_blk = selt_ref.shape[0] // NSA_KV_HEADS
    kc = SEL_KEY_CHUNK
    lanes = NSA_GROUP * qb
    s0 = pl.program_id(1) * qb
    n_full = s0 // kc
    win_keys = WINDOW + qb
    win_start = pl.multiple_of(jnp.maximum(s0 - WINDOW, 0), qb)

    blk_per_chunk = kc // SEL_BLOCK
    bias = jnp.where(selt_ref[...].astype(F32) > 0.5, 0.0, NEG)
    if blk_per_chunk < 8:
        bias_ref[...] = jnp.zeros(bias_ref.shape, F32)
    for c in range(bias_ref.shape[0]):
        bias_ref[c, :blk_per_chunk, :] = bias[c * blk_per_chunk:(c + 1) * blk_per_chunk]
    tq = s0 + lax.broadcasted_iota(jnp.int32, (1, qb), 1)
    dist = tq - (win_start + lax.broadcasted_iota(jnp.int32, (win_keys, qb), 0))
    wbias = jnp.where(lax.bitcast_convert_type(dist, jnp.uint32) < WINDOW, 0.0, NEG)
    wbias = jnp.concatenate([wbias] * NSA_GROUP, axis=1)
    gt = gatet_ref[...]

    groups = range(NSA_KV_HEADS)
    qg = [_pad_group(jnp.concatenate(
        [qt_ref[h * HEAD_DIM:(h + 1) * HEAD_DIM, :]
         for h in range(g * NSA_GROUP, (g + 1) * NSA_GROUP)], axis=1), g) for g in groups]
    onehot = jnp.where(
        jnp.right_shift(lax.broadcasted_iota(jnp.int32, (kc, D_KV), 0), SEL_SHIFT)
        == lax.broadcasted_iota(jnp.int32, (kc, D_KV), 1), 1.0, 0.0).astype(BF16)
    for g in groups:
        qaug_ref[g, :D_KV, :] = qg[g]
        qaug_ref[g, D_KV:, :] = jnp.zeros((D_KV, lanes), BF16)

    def scores(g, k0):
        rows = jnp.concatenate([bias_ref[g * (n_blk // blk_per_chunk) + k0 // kc],
                                jnp.zeros((8, qb), F32)], axis=0).astype(BF16)
        qaug_ref[g, D_KV:D_KV + 16, :] = jnp.concatenate([rows] * NSA_GROUP, axis=1)
        if not isinstance(k0, int):
            k0 = pl.multiple_of(k0, kc)
        keys = jnp.concatenate([kk_ref[pl.ds(k0, kc), :D_KV], onehot], axis=1)
        return _dot(keys, qaug_ref[g])

    def update(g, k0, s, m, l):
        m_new = jnp.maximum(m, jnp.max(s, axis=0, keepdims=True))
        alpha = jnp.exp2(m - m_new)
        p = jnp.exp2(s - m_new)
        v = vvt_ref[g * HEAD_DIM:(g + 1) * HEAD_DIM, pl.ds(k0, kc)]
        acc_ref[g] = alpha * acc_ref[g] + _dot(v, p.astype(BF16))
        return m_new, alpha * l + jnp.sum(p, axis=0, keepdims=True)

    def score_into(ref, c):
        for g in groups:
            ref[g] = scores(g, c * kc)

    def softmax_from(ref, c, carry):
        k0 = pl.multiple_of(c * kc, kc)
        return tuple(update(g, k0, ref[g], *carry[g]) for g in groups)

    acc_ref[...] = jnp.zeros(acc_ref.shape, F32)
    init = ((jnp.full((1, lanes), NEG, F32), jnp.zeros((1, lanes), F32)),) * NSA_KV_HEADS
    odd = lax.rem(n_full, 2)

    def peel_odd():
        score_into(sb_ref, 0)
        score_into(sa_ref, 1)
        return softmax_from(sb_ref, 0, init)

    def peel_even():
        score_into(sa_ref, 0)
        return init

    carry = lax.cond(odd == 1, peel_odd, peel_even)

    def body(i, carry):
        c = 2 * i + odd
        score_into(sb_ref, c + 1)
        carry = softmax_from(sa_ref, c, carry)
        score_into(sa_ref, c + 2)
        return softmax_from(sb_ref, c + 1, carry)

    carry = lax.fori_loop(0, n_full // 2, body, carry)
    k_last = pl.multiple_of(n_full * kc, kc)
    seen = k_last + lax.broadcasted_iota(jnp.int32, (kc, qb), 0) <= tq

    def last_scores(g):
        s = sa_ref[g]
        return jnp.concatenate([jnp.where(seen, s[:, r * qb:(r + 1) * qb], NEG)
                                for r in range(NSA_GROUP)], axis=1)

    def window_scores(g):
        return _dot(kk_ref[pl.ds(win_start, win_keys), D_KV:], qg[g]) + wbias

    def window_out(g, sw):
        pw = jnp.exp2(sw - jnp.max(sw, axis=0, keepdims=True))
        vw = vvt_ref[D_KV + g * HEAD_DIM:D_KV + (g + 1) * HEAD_DIM, pl.ds(win_start, win_keys)]
        return _dot(vw, pw.astype(BF16)) * (1.0 / jnp.sum(pw, axis=0, keepdims=True))

    o_sel, o_win, s_win = [], [], [None] * NSA_KV_HEADS
    for g in groups:
        s_win[g] = window_scores(g)
        _, l = update(g, k_last, last_scores(g), *carry[g])
        o_sel.append(acc_ref[g] * (1.0 / l))
    for g in groups:
        o_win.append(window_out(g, s_win[g]))

    pieces = []
    for g in groups:
        o_s, o_w = o_sel[g], o_win[g]
        for r in range(NSA_GROUP):
            h = g * NSA_GROUP + r
            ln = slice(r * qb, (r + 1) * qb)
            pieces.append(gt[3 * h:3 * h + 1] * ocmpt_ref[h * HEAD_DIM:(h + 1) * HEAD_DIM, :]
                          + gt[3 * h + 1:3 * h + 2] * o_s[:, ln]
                          + gt[3 * h + 2:3 * h + 3] * o_w[:, ln])
    out_ref[...] = jnp.concatenate(pieces, axis=0).T.astype(BF16)


def _sel_win(qt, kk, vvt, selt, ocmpt, gatet, batch, seq):
    qb = Q_BLOCK
    per = seq // qb
    cols = lambda height: pl.BlockSpec((height, qb), lambda b, i: (0, b * per + i))
    return pl.pallas_call(
        _sel_win_kernel,
        grid=(batch, per),
        in_specs=[cols(D_NSA),
                  pl.BlockSpec((seq, 2 * D_KV), lambda b, i: (b, 0)),
                  pl.BlockSpec((2 * D_KV, seq), lambda b, i: (0, b)),
                  cols(selt.shape[0]), cols(D_NSA), cols(128)],
        out_specs=pl.BlockSpec((qb, D_NSA), lambda b, i: (b * per + i, 0)),
        out_shape=jax.ShapeDtypeStruct((batch * seq, D_NSA), BF16),
        scratch_shapes=[pltpu.VMEM((selt.shape[0] * SEL_BLOCK // SEL_KEY_CHUNK, 8, qb), F32),
                        pltpu.VMEM((NSA_KV_HEADS, HEAD_DIM, NSA_GROUP * qb), F32),
                        pltpu.VMEM((NSA_KV_HEADS, 2 * D_KV, NSA_GROUP * qb), BF16),
                        pltpu.VMEM((NSA_KV_HEADS, SEL_KEY_CHUNK, NSA_GROUP * qb), F32),
                        pltpu.VMEM((NSA_KV_HEADS, SEL_KEY_CHUNK, NSA_GROUP * qb), F32)],
        compiler_params=pltpu.CompilerParams(
            dimension_semantics=("parallel", "arbitrary"), vmem_limit_bytes=VMEM_LIMIT),
        name="sel_win_attn",
    )(qt, kk, vvt, selt, ocmpt, gatet)


def kernel(x, ffn1_gate, ffn1_up, ffn1_down, ln1_g, ln1_b, w_in, conv_w, gmlp_ln_g, gmlp_ln_b, gmlp_w, gmlp_b, ck_pe, ck_w1, ck_b1, ck_w2, cv_pe, cv_w1, cv_b1, cv_w2, w_out, ln2_g, ln2_b, ffn2_gate, ffn2_up, ffn2_down, ln3_g, ln3_b):
    batch, seq, d = x.shape
    assert d == D_MODEL and seq % max(ROW_TILE, FFN_ROW_TILE, CMP_Q_TILE, SEL_KEY_CHUNK) == 0
    h = x.reshape(batch * seq, d)
    for l in range(ffn1_gate.shape[0]):
        h = _ffn_ln(h, ffn1_gate[l], ffn1_up[l], ffn1_down[l], ln1_g[l], ln1_b[l])
        yab, qt, kcr, vcr, kk, vvt, gatet = _proj(h, seq, w_in[l], conv_w[l], gmlp_ln_g[l],
                                                  gmlp_ln_b[l], gmlp_w[l], gmlp_b[l])
        kc, vct = _compress(kcr, vcr, batch, seq, ck_pe[l], ck_w1[l], ck_b1[l], ck_w2[l],
                            cv_pe[l], cv_w1[l], cv_b1[l], cv_w2[l])
        ocmpt, selt = _cmp_attn(qt, kc, vct, batch, seq)
        yc = _sel_win(qt, kk, vvt, selt, ocmpt, gatet, batch, seq)
        h = _ffn_ln(h, ffn2_gate[l], ffn2_up[l], ffn2_down[l], ln3_g[l], ln3_b[l],
                    mix=(yab, yc, w_out[l], ln2_g[l], ln2_b[l]))
    return h.reshape(batch, seq, d)
```
